```python
import math
import jax
import jax.numpy as jnp
from jax import lax
import numpy as np

D_MODEL = 2048
BATCH = 1
SEQ = 8192
DEPTH = 2
DEC_BATCH = 8
DEC_SEQ = 32
PAST_LEN = 1024

CHUNK = 64
NORM_EPS = 1e-5
NEG_INF = -1e30

A_HEAD_DIM = 64
A_HEADS = D_MODEL // A_HEAD_DIM
A_KV_HEADS = A_HEADS // 4
A_GROUP = A_HEADS // A_KV_HEADS
WINDOW = 128
WIN_CHUNKS = WINDOW // CHUNK
NUM_BUCKETS = 32
MAX_DISTANCE = 128

B_HEAD = 64
B_HEADS = D_MODEL // B_HEAD
B_DIM = B_HEADS * B_HEAD
B_DECAY_RANK = 64
B_ICLR_RANK = 64
B_GATE_RANK = 128
B_GN_EPS = 64e-5
B_PROJ = 3 * B_DIM + B_DECAY_RANK + B_ICLR_RANK + B_GATE_RANK
B_SPLITS = [B_DIM, 2 * B_DIM, 3 * B_DIM, 3 * B_DIM + B_DECAY_RANK, 3 * B_DIM + B_DECAY_RANK + B_ICLR_RANK]

C_DIM = D_MODEL
C_WIDTH = 3
D_DIM = D_MODEL
D_WIDTH = 31

N_BRANCH = 4
BRANCH_WIDTH = D_MODEL
A_Q = A_HEADS * A_HEAD_DIM
A_KV = A_KV_HEADS * A_HEAD_DIM
IN_SIZES = (A_Q, A_KV, A_KV, B_PROJ, C_DIM, C_DIM, C_DIM, 2 * D_DIM, N_BRANCH * D_MODEL)
IN_TOTAL = sum(IN_SIZES)
IN_SPLITS = [sum(IN_SIZES[:i + 1]) for i in range(len(IN_SIZES) - 1)]

N_EXPERTS = 32
TOP_K = 4
D_FF = D_MODEL
SWIGLU_LIMIT = 7.0
SWIGLU_ALPHA = 1.702
MOE_BLOCK = 128

kernel_name = 'streaming_hybrid_encoder_step'


def rms_norm(x, g):
    xf = x.astype(jnp.float32)
    y = xf * lax.rsqrt(jnp.mean(xf * xf, axis=-1, keepdims=True) + NORM_EPS) * g.astype(jnp.float32)
    return y.astype(x.dtype)


def layer_norm(x, g, b):
    xf = x.astype(jnp.float32)
    mu = jnp.mean(xf, axis=-1, keepdims=True)
    var = jnp.mean(jnp.square(xf - mu), axis=-1, keepdims=True)
    return ((xf - mu) * lax.rsqrt(var + NORM_EPS) * g.astype(jnp.float32) + b.astype(jnp.float32)).astype(x.dtype)


def t5_bucket(rel):
    nb = NUM_BUCKETS // 2
    max_exact = nb // 2
    ret = jnp.where(rel > 0, nb, 0)
    n = jnp.abs(rel)
    nf = jnp.maximum(n, 1).astype(jnp.float32)
    large = max_exact + (jnp.log(nf / max_exact) / math.log(MAX_DISTANCE / max_exact) * (nb - max_exact)).astype(jnp.int32)
    large = jnp.minimum(large, nb - 1)
    return ret + jnp.where(n < max_exact, n, large)


def t5_rel_bias(q_pos, k_pos, table):
    b = table[t5_bucket(k_pos[None, :] - q_pos[:, None])]
    b = jnp.transpose(b, (2, 0, 1)).astype(jnp.float32)
    return b.reshape(A_KV_HEADS, A_GROUP, q_pos.shape[0], k_pos.shape[0])


def chunk_visible(q_pos, k_pos):
    q = q_pos[..., :, None]
    k = k_pos[..., None, :]
    qc, kc = q // CHUNK, k // CHUNK
    return (k >= 0) & (kc <= qc) & (kc >= qc - WIN_CHUNKS)


def sink_attention(q, k, v, bias, mask, sinks):
    s = jnp.einsum('...qkgd,...skd->...kgqs', q, k).astype(jnp.float32) * (A_HEAD_DIM ** -0.5) + bias
    s = jnp.where(mask, s, NEG_INF)
    sink = sinks.astype(jnp.float32).reshape(A_KV_HEADS, A_GROUP, 1, 1)
    m = jnp.maximum(jnp.max(s, axis=-1, keepdims=True), sink)
    p = jnp.exp(s - m)
    p = p / (jnp.sum(p, axis=-1, keepdims=True) + jnp.exp(sink - m))
    return jnp.einsum('...kgqs,...skd->...qkgd', p.astype(v.dtype), v)


def swa_prompt(q, k, v, table, sinks):
    b, t = q.shape[:2]
    nc = t // CHUNK
    qb = q.reshape(b, nc, CHUNK, A_KV_HEADS, A_GROUP, A_HEAD_DIM)

    def band(z):
        zp = jnp.pad(z, ((0, 0), (WINDOW, 0), (0, 0), (0, 0))).reshape(b, nc + WIN_CHUNKS, CHUNK, A_KV_HEADS, A_HEAD_DIM)
        return jnp.concatenate([zp[:, i:i + nc] for i in range(WIN_CHUNKS + 1)], axis=2)

    kb, vb = band(k), band(v)
    blk = jnp.arange(nc)[:, None] * CHUNK
    q_pos = blk + jnp.arange(CHUNK)[None]
    k_pos = blk - WINDOW + jnp.arange(WINDOW + CHUNK)[None]
    mask = chunk_visible(q_pos, k_pos)[:, None, None]
    bias = t5_rel_bias(WINDOW + jnp.arange(CHUNK), jnp.arange(WINDOW + CHUNK), table)
    o = sink_attention(qb, kb, vb, bias, mask, sinks)
    return o.reshape(b, t, A_Q)


def swa_sample(q, k, v, k_cache, v_cache, table, sinks):
    b, sq = q.shape[:2]
    kk = jnp.concatenate([k_cache.astype(k.dtype), k], axis=1)
    vv = jnp.concatenate([v_cache.astype(v.dtype), v], axis=1)
    q_pos = PAST_LEN + jnp.arange(sq)
    k_pos = PAST_LEN - WINDOW + jnp.arange(WINDOW + sq)
    mask = chunk_visible(q_pos, k_pos)
    bias = t5_rel_bias(q_pos, k_pos, table)
    o = sink_attention(q, kk, vv, bias, mask, sinks)
    return o.reshape(b, sq, A_Q)


def causal_dwconv(u, prev, w):
    width = w.shape[0]
    full = jnp.concatenate([prev.astype(u.dtype), u], axis=1)
    y = lax.conv_general_dilated(full, w[:, None, :].astype(u.dtype), window_strides=(1,), padding='VALID',
                                 dimension_numbers=('NWC', 'WIO', 'NWC'), feature_group_count=u.shape[-1])
    return y, full[:, full.shape[1] - (width - 1):]


def rwkv7_mix(pb, shift_prev, wkv_prev, lp):
    f32 = jnp.float32
    b, t, _ = pb.shape
    prev = jnp.concatenate([shift_prev[:, None].astype(pb.dtype), pb[:, :-1]], axis=1)
    xm = pb + (prev - pb) * lp['mu']
    r, k, v, wl, al, gl = jnp.split(xm, B_SPLITS, axis=-1)
    logw = -jax.nn.softplus(-(lp['w0'] + jnp.tanh(wl) @ lp['w2']).astype(f32)) - 0.5
    decay = jnp.exp(-jnp.exp(logw))
    a = jax.nn.sigmoid((lp['a0'] + al @ lp['a2']).astype(f32))
    g = (jax.nn.sigmoid(gl) @ lp['g2']).astype(f32)
    heads = lambda z: z.astype(f32).reshape(b, t, B_HEADS, B_HEAD)
    r, k, v, decay, a = heads(r), heads(k), heads(v), heads(decay), heads(a)
    kk = k * lp['k_k'].astype(f32).reshape(B_HEADS, B_HEAD)
    kk = kk * lax.rsqrt(jnp.sum(kk * kk, axis=-1, keepdims=True) + 1e-12)
    k = k * (1.0 + (a - 1.0) * lp['k_a'].astype(f32).reshape(B_HEADS, B_HEAD))

    def step(S, inp):
        r_t, w_t, k_t, v_t, kk_t, a_t = inp
        sa = jnp.einsum('bhvk,bhk->bhv', S, -kk_t)
        S = S * w_t[:, :, None, :] + sa[..., None] * (kk_t * a_t)[:, :, None, :] + v_t[..., None] * k_t[:, :, None, :]
        return S, jnp.einsum('bhvk,bhk->bhv', S, r_t)

    tf = lambda z: jnp.swapaxes(z, 0, 1)
    S, y = lax.scan(step, wkv_prev.astype(f32), (tf(r), tf(decay), tf(k), tf(v), tf(kk), tf(a)))
    y = tf(y)
    mu = jnp.mean(y, axis=-1, keepdims=True)
    var = jnp.mean(jnp.square(y - mu), axis=-1, keepdims=True)
    yn = ((y - mu) * lax.rsqrt(var + B_GN_EPS)).reshape(b, t, B_DIM) * lp['ln_g'] + lp['ln_b']
    bonus = jnp.sum(r * k * lp['r_k'].astype(f32), axis=-1, keepdims=True) * v
    o = (yn + bonus.reshape(b, t, B_DIM)) * g
    return o.astype(pb.dtype), pb[:, -1], S.astype(wkv_prev.dtype)


def token_mixers(h, lp, table, st):
    b, t, _ = h.shape
    proj = h @ lp['w_in']
    qa, ka, va, pb, bg, cg, hc, pd, gl = jnp.split(proj, IN_SPLITS, axis=-1)
    qa = qa.reshape(b, t, A_KV_HEADS, A_GROUP, A_HEAD_DIM)
    ka = ka.reshape(b, t, A_KV_HEADS, A_HEAD_DIM)
    va = va.reshape(b, t, A_KV_HEADS, A_HEAD_DIM)
    if st['kv'] is None:
        o_a = swa_prompt(qa, ka, va, table, lp['sinks'])
        new_k, new_v = ka[:, t - WINDOW:], va[:, t - WINDOW:]
    else:
        o_a = swa_sample(qa, ka, va, st['kv'][0], st['kv'][1], table, lp['sinks'])
        new_k, new_v = ka, va
    o_b, new_shift, new_wkv = rwkv7_mix(pb, st['shift'], st['wkv'], lp)
    yc, new_sconv = causal_dwconv(cg * hc, st['sconv'], lp['sconv_w'])
    o_c = bg * yc
    ga, gb = jnp.split(pd, 2, axis=-1)
    yd, new_cconv = causal_dwconv(ga * jax.nn.sigmoid(gb), st['cconv'], lp['cconv_w'])
    o_d = jax.nn.silu(layer_norm(yd + lp['cconv_b'], lp['cnorm_g'], lp['cnorm_b']))
    branches = jnp.stack([o_a, o_b, o_c, o_d], axis=2)
    gates = jax.nn.sigmoid(gl.reshape(b, t, N_BRANCH, D_MODEL))
    merged = jnp.sum(gates * jnp.einsum('btnw,nwd->btnd', branches, lp['w_branch']), axis=2)
    return merged @ lp['w_out'], (new_k, new_v, new_wkv, new_shift, new_sconv, new_cconv)


def moe_ffn(h, router_w, router_b, w_gu, b_gu, w_down, b_down):
    n, d = h.shape
    logits = jnp.dot(h, router_w).astype(jnp.float32) + router_b.astype(jnp.float32)
    top_val, top_idx = lax.top_k(logits, TOP_K)
    probs = jax.nn.softmax(top_val, axis=-1)
    flat_e = top_idx.reshape(-1)
    order = jnp.argsort(flat_e, stable=True)
    e_sorted = flat_e[order]
    tok_sorted = order // TOP_K
    p_sorted = probs.reshape(-1)[order]
    counts = jnp.bincount(flat_e, length=N_EXPERTS)
    padded = (counts + MOE_BLOCK - 1) // MOE_BLOCK * MOE_BLOCK
    pad_end = jnp.cumsum(padded)
    pad_start = pad_end - padded
    start = jnp.cumsum(counts) - counts
    dest = pad_start[e_sorted] + jnp.arange(n * TOP_K, dtype=jnp.int32) - start[e_sorted]
    n_blocks = -(-(n * TOP_K + N_EXPERTS * (MOE_BLOCK - 1)) // MOE_BLOCK)
    row_tok = jnp.full((n_blocks * MOE_BLOCK,), n, jnp.int32).at[dest].set(tok_sorted.astype(jnp.int32))
    block_e = jnp.minimum(jnp.searchsorted(pad_end, jnp.arange(n_blocks) * MOE_BLOCK, side='right'), N_EXPERTS - 1)
    xb = jnp.concatenate([h, jnp.zeros((1, d), h.dtype)], axis=0)[row_tok].reshape(n_blocks, MOE_BLOCK, d)

    def expert_block(args):
        xs, e = args
        gu = xs @ w_gu[e] + b_gu[e]
        gate, up = jnp.split(gu, 2, axis=-1)
        gate = jnp.minimum(gate, SWIGLU_LIMIT)
        up = jnp.clip(up, -SWIGLU_LIMIT, SWIGLU_LIMIT)
        act = gate * jax.nn.sigmoid(SWIGLU_ALPHA * gate) * (up + 1)
        return act @ w_down[e] + b_down[e]

    yb = lax.map(expert_block, (xb, block_e)).reshape(n_blocks * MOE_BLOCK, d)
    contrib = yb[dest] * p_sorted[:, None].astype(h.dtype)
    return jnp.zeros_like(h).at[tok_sorted].add(contrib)


def layer(x, c, lp, table, st):
    mod = (jax.nn.silu(c) @ lp['w_mod'] + lp['b_mod'])[:, None, :]
    sh1, sc1, g1, sh2, sc2, g2 = jnp.split(mod, 6, axis=-1)
    h = rms_norm(x, lp['norm1_g']) * (1 + sc1) + sh1
    mix, new_st = token_mixers(h, lp, table, st)
    x = x + g1 * mix
    h = rms_norm(x, lp['norm2_g']) * (1 + sc2) + sh2
    b, t, d = h.shape
    ffn = moe_ffn(h.reshape(b * t, d), lp['router_w'], lp['router_b'], lp['w_gu'], lp['b_gu'],
                  lp['w_down'], lp['b_down']).reshape(b, t, d)
    return x + g2 * ffn, new_st


def setup_inputs(seed: int = 0) -> dict:
    key = jax.random.key(seed)
    ks = iter(jax.random.split(key, 64))
    f32 = jnp.float32

    def nrm(shape, scale):
        return jax.random.normal(next(ks), shape, f32) * scale

    def uni(shape, lo, hi):
        return jax.random.uniform(next(ks), shape, f32, lo, hi)

    L, D = DEPTH, D_MODEL
    return {
        'x_prompt': nrm((BATCH, SEQ, D), 1.0),
        'x_sample': nrm((DEC_BATCH, DEC_SEQ, D), 1.0),
        'c_prompt': nrm((BATCH, D), 1.0),
        'c_sample': nrm((DEC_BATCH, D), 1.0),
        'cache_k': nrm((L, DEC_BATCH, WINDOW, A_KV_HEADS, A_HEAD_DIM), 1.0),
        'cache_v': nrm((L, DEC_BATCH, WINDOW, A_KV_HEADS, A_HEAD_DIM), 1.0),
        'state_wkv': nrm((L, DEC_BATCH, B_HEADS, B_HEAD, B_HEAD), 0.3),
        'state_shift': nrm((L, DEC_BATCH, B_PROJ), 1.0),
        'state_sconv': nrm((L, DEC_BATCH, C_WIDTH - 1, C_DIM), 0.5),
        'state_cconv': nrm((L, DEC_BATCH, D_WIDTH - 1, D_DIM), 0.5),
        'w_mod': nrm((L, D, 6 * D), 0.5 * D ** -0.5),
        'b_mod': nrm((L, 6 * D), 0.02),
        'norm1_g': 1.0 + nrm((L, D), 0.05),
        'norm2_g': 1.0 + nrm((L, D), 0.05),
        'w_in': nrm((L, D, IN_TOTAL), D ** -0.5),
        'attn_sinks': nrm((L, A_HEADS), 0.5),
        'rel_bias_table': nrm((NUM_BUCKETS, A_HEADS), 0.5),
        'rwkv_mu': uni((L, B_PROJ), 0.0, 1.0),
        'rwkv_w0': uni((L, B_DIM), -3.0, 0.0),
        'rwkv_w2': nrm((L, B_DECAY_RANK, B_DIM), 0.5 * B_DECAY_RANK ** -0.5),
        'rwkv_a0': nrm((L, B_DIM), 0.1),
        'rwkv_a2': nrm((L, B_ICLR_RANK, B_DIM), 0.5 * B_ICLR_RANK ** -0.5),
        'rwkv_g2': nrm((L, B_GATE_RANK, B_DIM), B_GATE_RANK ** -0.5),
        'rwkv_k_k': 0.85 + nrm((L, B_DIM), 0.05),
        'rwkv_k_a': 1.0 + nrm((L, B_DIM), 0.05),
        'rwkv_r_k': nrm((L, B_HEADS, B_HEAD), 0.1),
        'rwkv_ln_g': 1.0 + nrm((L, B_DIM), 0.05),
        'rwkv_ln_b': nrm((L, B_DIM), 0.02),
        'sconv_w': nrm((L, C_WIDTH, C_DIM), C_WIDTH ** -0.5),
        'cconv_w': nrm((L, D_WIDTH, D_DIM), D_WIDTH ** -0.5),
        'cconv_b': nrm((L, D_DIM), 0.02),
        'cnorm_g': 1.0 + nrm((L, D_DIM), 0.05),
        'cnorm_b': nrm((L, D_DIM), 0.02),
        'w_branch': nrm((L, N_BRANCH, BRANCH_WIDTH, D), BRANCH_WIDTH ** -0.5),
        'w_out': nrm((L, D, D), D ** -0.5),
        'router_w': nrm((L, D, N_EXPERTS), D ** -0.5),
        'router_b': nrm((L, N_EXPERTS), 0.01),
        'expert_w_gu': nrm((L, N_EXPERTS, D, 2 * D_FF), D ** -0.5),
        'expert_b_gu': nrm((L, N_EXPERTS, 2 * D_FF), 0.01),
        'expert_w_down': nrm((L, N_EXPERTS, D_FF, D), D_FF ** -0.5),
        'expert_b_down': nrm((L, N_EXPERTS, D), 0.01),
        'final_g': 1.0 + nrm((D,), 0.05),
    }


def reference(x_prompt, x_sample, c_prompt, c_sample, cache_k, cache_v, state_wkv, state_shift, state_sconv,
              state_cconv, w_mod, b_mod, norm1_g, norm2_g, w_in, attn_sinks, rel_bias_table, rwkv_mu, rwkv_w0,
              rwkv_w2, rwkv_a0, rwkv_a2, rwkv_g2, rwkv_k_k, rwkv_k_a, rwkv_r_k, rwkv_ln_g, rwkv_ln_b, sconv_w,
              cconv_w, cconv_b, cnorm_g, cnorm_b, w_branch, w_out, router_w, router_b, expert_w_gu, expert_b_gu,
              expert_w_down, expert_b_down, final_g):
    xp, xs = x_prompt, x_sample
    bp = xp.shape[0]
    st_p, st_s = [], []
    for l in range(DEPTH):
        lp = dict(w_mod=w_mod[l], b_mod=b_mod[l], norm1_g=norm1_g[l], norm2_g=norm2_g[l], w_in=w_in[l],
                  sinks=attn_sinks[l], mu=rwkv_mu[l], w0=rwkv_w0[l], w2=rwkv_w2[l], a0=rwkv_a0[l], a2=rwkv_a2[l],
                  g2=rwkv_g2[l], k_k=rwkv_k_k[l], k_a=rwkv_k_a[l], r_k=rwkv_r_k[l], ln_g=rwkv_ln_g[l],
                  ln_b=rwkv_ln_b[l], sconv_w=sconv_w[l], cconv_w=cconv_w[l], cconv_b=cconv_b[l],
                  cnorm_g=cnorm_g[l], cnorm_b=cnorm_b[l], w_branch=w_branch[l], w_out=w_out[l],
                  router_w=router_w[l], router_b=router_b[l], w_gu=expert_w_gu[l], b_gu=expert_b_gu[l],
                  w_down=expert_w_down[l], b_down=expert_b_down[l])
        prompt_state = dict(kv=None,
                            shift=jnp.zeros((bp, B_PROJ), xp.dtype),
                            wkv=jnp.zeros((bp, B_HEADS, B_HEAD, B_HEAD), xp.dtype),
                            sconv=jnp.zeros((bp, C_WIDTH - 1, C_DIM), xp.dtype),
                            cconv=jnp.zeros((bp, D_WIDTH - 1, D_DIM), xp.dtype))
        sample_state = dict(kv=(cache_k[l], cache_v[l]), shift=state_shift[l], wkv=state_wkv[l],
                            sconv=state_sconv[l], cconv=state_cconv[l])
        xp, sp = layer(xp, c_prompt, lp, rel_bias_table, prompt_state)
        xs, ss = layer(xs, c_sample, lp, rel_bias_table, sample_state)
        st_p.append(sp)
        st_s.append(ss)
    y_prompt = rms_norm(xp, final_g)
    y_sample = rms_norm(xs, final_g)
    return (y_prompt, y_sample,
            jnp.stack([s[0] for s in st_p]), jnp.stack([s[1] for s in st_p]), jnp.stack([s[2] for s in st_p]),
            jnp.stack([s[3] for s in st_p]), jnp.stack([s[4] for s in st_p]), jnp.stack([s[5] for s in st_p]),
            jnp.stack([s[0] for s in st_s]), jnp.stack([s[1] for s in st_s]), jnp.stack([s[2] for s in st_s]),
            jnp.stack([s[3] for s in st_s]), jnp.stack([s[4] for s in st_s]), jnp.stack([s[5] for s in st_s]))
```

```python
import functools
import math

import jax
import jax.numpy as jnp
from jax import lax
from jax.experimental import pallas as pl
from jax.experimental.pallas import tpu as pltpu

F32 = jnp.float32
BF16 = jnp.bfloat16

D_MODEL = 2048
SEQ = 8192
DEPTH = 2
DEC_BATCH = 8
DEC_SEQ = 32
PAST_LEN = 1024
CHUNK = 64
NORM_EPS = 1e-5
NEG_INF = -1e30

A_HEAD_DIM = 64
A_HEADS = 32
A_KV_HEADS = 8
A_GROUP = 4
WINDOW = 128
WIN_CHUNKS = 2
NUM_BUCKETS = 32
MAX_DISTANCE = 128
A_Q = A_HEADS * A_HEAD_DIM
A_KV = A_KV_HEADS * A_HEAD_DIM

B_HEAD = 64
B_HEADS = 32
B_DIM = 2048
B_DECAY_RANK = 64
B_ICLR_RANK = 64
B_GATE_RANK = 128
B_GN_EPS = 64e-5
B_PROJ = 3 * B_DIM + B_DECAY_RANK + B_ICLR_RANK + B_GATE_RANK

C_WIDTH = 3
D_WIDTH = 31
N_BRANCH = 4

N_EXPERTS = 32
TOP_K = 4
D_FF = D_MODEL
SWIGLU_LIMIT = 7.0
SWIGLU_ALPHA = 1.702

OFF_Q = 0
OFF_K = OFF_Q + A_Q
OFF_V = OFF_K + A_KV
OFF_PB = OFF_V + A_KV
OFF_BG = OFF_PB + B_PROJ
OFF_CG = OFF_BG + D_MODEL
OFF_HC = OFF_CG + D_MODEL
OFF_PD = OFF_HC + D_MODEL
OFF_GL = OFF_PD + 2 * D_MODEL
IN_TOTAL = OFF_GL + N_BRANCH * D_MODEL

N_PROMPT = SEQ
N_SAMPLE = DEC_BATCH * DEC_SEQ
N_TOK = N_PROMPT + N_SAMPLE
MOD_GROUP = 32
N_MOD_GROUPS = N_TOK // MOD_GROUP

LANES = 128
RWKV_T = 64
RWKV_PAIRS = 4
RWKV_SEQ_T = 16
RWKV_TAIL = 512
MOE_BLK = 256
MOE_FF_TILE = 512
VMEM_LIMIT = 48 * 1024 * 1024
MOE_VMEM_LIMIT = 56 * 1024 * 1024


def _cparams(sem):
    return pltpu.CompilerParams(dimension_semantics=sem, vmem_limit_bytes=VMEM_LIMIT)


def _dot(a, b):
    return jnp.dot(a, b, preferred_element_type=F32)


def _dot_nt(a, b):
    return lax.dot_general(a, b, (((1,), (1,)), ((), ())), preferred_element_type=F32)


def _mm_kernel(x_ref, w_ref, o_ref, *, act):
    y = _dot(x_ref[...], w_ref[...])
    if act == "sigmoid":
        y = jax.nn.sigmoid(y)
    o_ref[...] = y.astype(o_ref.dtype)


def matmul(x, w, *, tm, tn, out_dtype=F32, act=None):
    m, k = x.shape
    n = w.shape[1]
    assert m % tm == 0 and n % tn == 0
    return pl.pallas_call(
        functools.partial(_mm_kernel, act=act),
        grid=(m // tm, n // tn),
        in_specs=[pl.BlockSpec((tm, k), lambda i, j: (i, 0)),
                  pl.BlockSpec((k, tn), lambda i, j: (0, j))],
        out_specs=pl.BlockSpec((tm, tn), lambda i, j: (i, j)),
        out_shape=jax.ShapeDtypeStruct((m, n), out_dtype),
        compiler_params=_cparams(("parallel", "parallel")),
    )(x, w)


def _group_rows(gi):
    return slice(gi * MOD_GROUP, (gi + 1) * MOD_GROUP)


def _norm_mod_rows(x_ref, g_ref, sh_ref, sc_ref, gi):
    x = x_ref[_group_rows(gi), :]
    y = x * lax.rsqrt(jnp.mean(x * x, axis=-1, keepdims=True) + NORM_EPS) * g_ref[...]
    return y * (1.0 + sc_ref[gi:gi + 1, :]) + sh_ref[gi:gi + 1, :]


def _norm_mod_kernel(x_ref, g_ref, sh_ref, sc_ref, h_ref):
    for gi in range(x_ref.shape[0] // MOD_GROUP):
        h_ref[_group_rows(gi), :] = _norm_mod_rows(x_ref, g_ref, sh_ref, sc_ref, gi).astype(h_ref.dtype)


def _norm_mod_router_kernel(x_ref, g_ref, sh_ref, sc_ref, rw_ref, rb_ref, h_ref, lg_ref):
    w = rw_ref[...].astype(BF16)
    for gi in range(x_ref.shape[0] // MOD_GROUP):
        h = _norm_mod_rows(x_ref, g_ref, sh_ref, sc_ref, gi).astype(BF16)
        h_ref[_group_rows(gi), :] = h
        lg_ref[_group_rows(gi), :] = _dot(h, w) + rb_ref[...]


def norm_mod(x, g, mod_g, shift_idx, scale_idx, router=None, tm=256):
    m, d = x.shape
    gm = tm // MOD_GROUP
    in_specs = [pl.BlockSpec((tm, d), lambda i: (i, 0)),
                pl.BlockSpec((1, d), lambda i: (0, 0)),
                pl.BlockSpec((gm, d), lambda i: (i, shift_idx)),
                pl.BlockSpec((gm, d), lambda i: (i, scale_idx))]
    args = [x, g.reshape(1, d), mod_g, mod_g]
    if router is None:
        return pl.pallas_call(
            _norm_mod_kernel, grid=(m // tm,), in_specs=in_specs,
            out_specs=pl.BlockSpec((tm, d), lambda i: (i, 0)),
            out_shape=jax.ShapeDtypeStruct((m, d), BF16),
            compiler_params=_cparams(("parallel",)))(*args)
    rw, rb = router
    ne = rw.shape[1]
    in_specs += [pl.BlockSpec((d, ne), lambda i: (0, 0)), pl.BlockSpec((1, ne), lambda i: (0, 0))]
    args += [rw, rb.reshape(1, ne)]
    return pl.pallas_call(
        _norm_mod_router_kernel, grid=(m // tm,), in_specs=in_specs,
        out_specs=[pl.BlockSpec((tm, d), lambda i: (i, 0)), pl.BlockSpec((tm, ne), lambda i: (i, 0))],
        out_shape=[jax.ShapeDtypeStruct((m, d), BF16), jax.ShapeDtypeStruct((m, ne), F32)],
        compiler_params=_cparams(("parallel",)))(*args)


def _final_norm_kernel(x_ref, g_ref, o_ref):
    x = x_ref[...]
    o_ref[...] = x * lax.rsqrt(jnp.mean(x * x, axis=-1, keepdims=True) + NORM_EPS) * g_ref[...]


def final_norm(x, g, tm=256):
    m, d = x.shape
    return pl.pallas_call(
        _final_norm_kernel, grid=(m // tm,),
        in_specs=[pl.BlockSpec((tm, d), lambda i: (i, 0)), pl.BlockSpec((1, d), lambda i: (0, 0))],
        out_specs=pl.BlockSpec((tm, d), lambda i: (i, 0)),
        out_shape=jax.ShapeDtypeStruct((m, d), F32),
        compiler_params=_cparams(("parallel",)))(x, g.reshape(1, d))


def _mm_resid_kernel(a_ref, w_ref, x_ref, gate_ref, o_ref):
    y = _dot(a_ref[...], w_ref[...])
    for gi in range(y.shape[0] // MOD_GROUP):
        rows = _group_rows(gi)
        o_ref[rows, :] = x_ref[rows, :] + gate_ref[gi:gi + 1, :] * y[rows]


def matmul_resid(a, w, x, mod_g, gate_idx, *, tm=768, tn=512):
    m, k = a.shape
    n = w.shape[1]
    gm = tm // MOD_GROUP
    nj = n // tn
    return pl.pallas_call(
        _mm_resid_kernel, grid=(m // tm, nj),
        in_specs=[pl.BlockSpec((tm, k), lambda i, j: (i, 0)),
                  pl.BlockSpec((k, tn), lambda i, j: (0, j)),
                  pl.BlockSpec((tm, tn), lambda i, j: (i, j)),
                  pl.BlockSpec((gm, tn), lambda i, j: (i, gate_idx * nj + j))],
        out_specs=pl.BlockSpec((tm, tn), lambda i, j: (i, j)),
        out_shape=jax.ShapeDtypeStruct((m, n), F32),
        compiler_params=_cparams(("parallel", "parallel")))(a, w, x, mod_g)


def _resid_kernel(x_ref, f_ref, gate_ref, o_ref):
    for gi in range(x_ref.shape[0] // MOD_GROUP):
        rows = _group_rows(gi)
        o_ref[rows, :] = x_ref[rows, :] + gate_ref[gi:gi + 1, :] * f_ref[rows, :]


def resid_gate(x, f, mod_g, gate_idx, tm=256):
    m, d = x.shape
    gm = tm // MOD_GROUP
    return pl.pallas_call(
        _resid_kernel, grid=(m // tm,),
        in_specs=[pl.BlockSpec((tm, d), lambda i: (i, 0)), pl.BlockSpec((tm, d), lambda i: (i, 0)),
                  pl.BlockSpec((gm, d), lambda i: (i, gate_idx))],
        out_specs=pl.BlockSpec((tm, d), lambda i: (i, 0)),
        out_shape=jax.ShapeDtypeStruct((m, d), F32),
        compiler_params=_cparams(("parallel",)))(x, f, mod_g)


def _merge_kernel(b_ref, w_ref, g_ref, o_ref, acc_ref):
    n = pl.program_id(2)
    part = g_ref[...] * _dot(b_ref[0], w_ref[0])

    @pl.when(n == 0)
    def _():
        acc_ref[...] = part

    @pl.when(n > 0)
    def _():
        acc_ref[...] += part

    @pl.when(n == N_BRANCH - 1)
    def _():
        o_ref[...] = acc_ref[...].astype(o_ref.dtype)


def merge_branches(branches, w_branch, gates, *, tm=768, tn=512):
    nb, m, k = branches.shape
    d = w_branch.shape[2]
    nj = d // tn
    return pl.pallas_call(
        _merge_kernel, grid=(m // tm, nj, nb),
        in_specs=[pl.BlockSpec((1, tm, k), lambda i, j, n: (n, i, 0)),
                  pl.BlockSpec((1, k, tn), lambda i, j, n: (n, 0, j)),
                  pl.BlockSpec((tm, tn), lambda i, j, n: (i, n * nj + j))],
        out_specs=pl.BlockSpec((tm, tn), lambda i, j, n: (i, j)),
        out_shape=jax.ShapeDtypeStruct((m, d), BF16),
        scratch_shapes=[pltpu.VMEM((tm, tn), F32)],
        compiler_params=_cparams(("parallel", "parallel", "arbitrary")))(branches, w_branch, gates)


def _attn_core(q, k, v, bias_ref, sink_ref, o_ref, key_ok):
    sq = q.shape[0]
    lane = lax.broadcasted_iota(jnp.int32, (1, LANES), 1)
    low = lane < A_HEAD_DIM
    q = q * (A_HEAD_DIM ** -0.5)
    for kvp in range(A_KV_HEADS // 2):
        kp = k[:, kvp * LANES:(kvp + 1) * LANES]
        vp = v[:, kvp * LANES:(kvp + 1) * LANES]
        kp_sw = pltpu.roll(kp, A_HEAD_DIM, 1)
        vp_sw = pltpu.roll(vp, A_HEAD_DIM, 1)
        for half in range(2):
            g = 2 * kvp + half
            if half == 0:
                k2 = jnp.where(low, kp, kp_sw)
                v2 = jnp.where(low, vp, vp_sw)
            else:
                k2 = jnp.where(low, kp_sw, kp)
                v2 = jnp.where(low, vp_sw, vp)
            parts = []
            for j in range(2):
                qp = q[:, (2 * g + j) * LANES:(2 * g + j + 1) * LANES]
                parts.append(jnp.where(low, qp, 0.0))
                parts.append(jnp.where(low, 0.0, qp))
            lhs = jnp.concatenate(parts, axis=0).astype(BF16)
            s = _dot_nt(lhs, k2.astype(BF16)) + bias_ref[g]
            if key_ok is not None:
                s = jnp.where(key_ok, s, NEG_INF)
            sink = sink_ref[g]
            m = jnp.maximum(jnp.max(s, axis=-1, keepdims=True), sink)
            p = jnp.exp(s - m)
            p = p / (jnp.sum(p, axis=-1, keepdims=True) + jnp.exp(sink - m))
            o2 = _dot(p.astype(BF16), v2.astype(BF16))
            for j in range(2):
                oj = jnp.where(low, o2[(2 * j) * sq:(2 * j + 1) * sq], o2[(2 * j + 1) * sq:(2 * j + 2) * sq])
                o_ref[:, (2 * g + j) * LANES:(2 * g + j + 1) * LANES] = oj.astype(o_ref.dtype)


def _attn_prompt_kernel(q_ref, k0_ref, k1_ref, k2_ref, v0_ref, v1_ref, v2_ref, bias_ref, sink_ref, o_ref):
    c = pl.program_id(0)
    k = jnp.concatenate([k0_ref[...], k1_ref[...], k2_ref[...]], axis=0)
    v = jnp.concatenate([v0_ref[...], v1_ref[...], v2_ref[...]], axis=0)
    sk = k.shape[0]
    k_pos = (c - WIN_CHUNKS) * CHUNK + lax.broadcasted_iota(jnp.int32, (1, sk), 1)
    _attn_core(q_ref[...], k, v, bias_ref, sink_ref, o_ref, k_pos >= 0)


_QKV_KCOL = A_Q // A_KV
_QKV_VCOL = _QKV_KCOL + 1


def attn_prompt(qkv, bias, sink_col, n_prompt):
    nc = n_prompt // CHUNK
    kspec = lambda d, col: pl.BlockSpec((CHUNK, A_KV), lambda c: (jnp.maximum(c - d, 0), col))
    return pl.pallas_call(
        _attn_prompt_kernel, grid=(nc,),
        in_specs=[pl.BlockSpec((CHUNK, A_Q), lambda c: (c, 0)),
                  kspec(2, _QKV_KCOL), kspec(1, _QKV_KCOL), kspec(0, _QKV_KCOL),
                  kspec(2, _QKV_VCOL), kspec(1, _QKV_VCOL), kspec(0, _QKV_VCOL),
                  pl.BlockSpec(bias.shape, lambda c: (0, 0, 0)),
                  pl.BlockSpec(sink_col.shape, lambda c: (0, 0, 0))],
        out_specs=pl.BlockSpec((CHUNK, A_Q), lambda c: (c, 0)),
        out_shape=jax.ShapeDtypeStruct((n_prompt, A_Q), BF16),
        compiler_params=_cparams(("parallel",)))(qkv, qkv, qkv, qkv, qkv, qkv, qkv, bias, sink_col)


def _attn_sample_kernel(q_ref, kn_ref, vn_ref, kc_ref, vc_ref, bias_ref, sink_ref, o_ref):
    k = jnp.concatenate([kc_ref[0], kn_ref[...]], axis=0)
    v = jnp.concatenate([vc_ref[0], vn_ref[...]], axis=0)
    _attn_core(q_ref[...], k, v, bias_ref, sink_ref, o_ref, None)


def attn_sample(qkv, k_cache, v_cache, bias, sink_col, n_prompt):
    nb = k_cache.shape[0]
    base = n_prompt // DEC_SEQ
    return pl.pallas_call(
        _attn_sample_kernel, grid=(nb,),
        in_specs=[pl.BlockSpec((DEC_SEQ, A_Q), lambda b: (base + b, 0)),
                  pl.BlockSpec((DEC_SEQ, A_KV), lambda b: (base + b, _QKV_KCOL)),
                  pl.BlockSpec((DEC_SEQ, A_KV), lambda b: (base + b, _QKV_VCOL)),
                  pl.BlockSpec((1, WINDOW, A_KV), lambda b: (b, 0, 0)),
                  pl.BlockSpec((1, WINDOW, A_KV), lambda b: (b, 0, 0)),
                  pl.BlockSpec(bias.shape, lambda b: (0, 0, 0)),
                  pl.BlockSpec(sink_col.shape, lambda b: (0, 0, 0))],
        out_specs=pl.BlockSpec((DEC_SEQ, A_Q), lambda b: (b, 0)),
        out_shape=jax.ShapeDtypeStruct((nb * DEC_SEQ, A_Q), BF16),
        compiler_params=_cparams(("parallel",)))(qkv, qkv, qkv, k_cache, v_cache, bias, sink_col)


def _t5_bucket(rel):
    nb = NUM_BUCKETS // 2
    max_exact = nb // 2
    ret = jnp.where(rel > 0, nb, 0)
    n = jnp.abs(rel)
    nf = jnp.maximum(n, 1).astype(F32)
    large = max_exact + (jnp.log(nf / max_exact) / math.log(MAX_DISTANCE / max_exact) * (nb - max_exact)).astype(jnp.int32)
    large = jnp.minimum(large, nb - 1)
    return ret + jnp.where(n < max_exact, n, large)


def _attn_tables(q_pos, k_pos, table, sinks, mask_positions):
    sq, sk = q_pos.shape[0], k_pos.shape[0]
    b = table[_t5_bucket(k_pos[None, :] - q_pos[:, None])]
    b = jnp.transpose(b, (2, 0, 1)).astype(F32)
    if mask_positions:
        qc, kc = q_pos[:, None] // CHUNK, k_pos[None, :] // CHUNK
        vis = (k_pos[None, :] >= 0) & (kc <= qc) & (kc >= qc - WIN_CHUNKS)
        b = jnp.where(vis[None], b, NEG_INF)
    bias = b.reshape(A_KV_HEADS, A_GROUP * sq, sk)
    sink_col = jnp.broadcast_to(sinks.astype(F32).reshape(A_KV_HEADS, A_GROUP, 1, 1),
                                (A_KV_HEADS, A_GROUP, sq, 1)).reshape(A_KV_HEADS, A_GROUP * sq, 1)
    return bias, sink_col


def _rwkv_kernel(r_ref, lw_ref, k_ref, v_ref, kk_ref, a_ref, s0_ref, y_ref, st_ref, s_scr):
    c = pl.program_id(2)
    t = RWKV_T

    @pl.when(c == 0)
    def _():
        s_scr[...] = s0_ref[0]

    row = lax.broadcasted_iota(jnp.int32, (t, LANES), 0)
    lane = lax.broadcasted_iota(jnp.int32, (t, LANES), 1)
    low = lane < B_HEAD
    col = lane & (t - 1)
    strict = col < row
    incl = col <= row
    tri = jnp.where(lax.broadcasted_iota(jnp.int32, (t, t), 1) <= lax.broadcasted_iota(jnp.int32, (t, t), 0),
                    1.0, 0.0).astype(BF16)
    r128 = lax.broadcasted_iota(jnp.int32, (LANES, LANES), 0)
    c128 = lax.broadcasted_iota(jnp.int32, (LANES, LANES), 1)
    same_head = (r128 < B_HEAD) == (c128 < B_HEAD)

    def stack(x):
        return jnp.concatenate([jnp.where(low, x, 0.0), jnp.where(low, 0.0, x)], axis=0)

    for p in range(RWKV_PAIRS):
        sl = slice(p * LANES, (p + 1) * LANES)
        lw = lw_ref[0, :, sl]
        hi = lw.astype(BF16)
        rem = lw - hi.astype(F32)
        mid = rem.astype(BF16)
        lo = (rem - mid.astype(F32)).astype(BF16)
        cs = _dot(tri, hi) + _dot(tri, mid) + _dot(tri, lo)
        cs_end = cs[t - 1:t, :]
        kk = kk_ref[0, :, sl]
        ka = kk * a_ref[0, :, sl]
        kx = k_ref[0, :, sl]
        g_inv = jnp.exp(-cs)
        g_end = jnp.exp(cs_end - cs)
        a_t = -kk * jnp.exp(cs - lw)
        r_t = r_ref[0, :, sl] * jnp.exp(cs)
        v = v_ref[0, :, sl]
        lhs = jnp.concatenate([a_t, r_t], axis=0).astype(BF16)
        rhs = jnp.concatenate([stack(ka * g_inv), stack(kx * g_inv)], axis=0).astype(BF16)
        d1 = _dot_nt(lhs, rhs)
        s = s_scr[p]
        d2 = _dot_nt(lhs, s.astype(BF16))
        l_ab = jnp.where(strict, d1[:t, :2 * t], 0.0)
        l_ak = jnp.where(strict, d1[:t, 2 * t:], 0.0)
        l_rb = jnp.where(incl, d1[t:, :2 * t], 0.0)
        l_rk = jnp.where(incl, d1[t:, 2 * t:], 0.0)
        st_v = stack(v).astype(BF16)
        u = d2[:t] + _dot(l_ak.astype(BF16), st_v)
        pw = l_ab
        n_steps = int(math.log2(t))
        for i in range(n_steps):
            u = u + _dot(pw.astype(BF16), stack(u).astype(BF16))
            if i < n_steps - 1:
                pw = _dot(pw.astype(BF16), stack(pw).astype(BF16))
        y = d2[t:] + _dot(jnp.concatenate([l_rb, l_rk], axis=1).astype(BF16),
                          jnp.concatenate([stack(u).astype(BF16), st_v], axis=0))
        uv_t = jnp.concatenate([u, v], axis=0).T.astype(BF16)
        upd = _dot(uv_t, jnp.concatenate([ka * g_end, kx * g_end], axis=0).astype(BF16))
        s_scr[p] = s * jnp.exp(cs_end) + jnp.where(same_head, upd, 0.0)
        y_ref[0, :, sl] = y

    @pl.when(c == pl.num_programs(2) - 1)
    def _():
        st_ref[0] = s_scr[...]


def rwkv_scan(r, lw, k, v, kk, a, s0):
    b, l, d = r.shape
    npair = d // LANES
    ng = npair // RWKV_PAIRS
    w = RWKV_PAIRS * LANES
    seq = pl.BlockSpec((1, RWKV_T, w), lambda bi, gi, ci: (bi, ci, gi))
    sspec = pl.BlockSpec((1, RWKV_PAIRS, LANES, LANES), lambda bi, gi, ci: (bi, gi, 0, 0))
    return pl.pallas_call(
        _rwkv_kernel, name="rwkv_chunk", grid=(b, ng, l // RWKV_T),
        in_specs=[seq] * 6 + [sspec],
        out_specs=[seq, sspec],
        out_shape=[jax.ShapeDtypeStruct((b, l, d), F32), jax.ShapeDtypeStruct(s0.shape, F32)],
        scratch_shapes=[pltpu.VMEM((RWKV_PAIRS, LANES, LANES), F32)],
        compiler_params=_cparams(("parallel", "parallel", "arbitrary")))(r, lw, k, v, kk, a, s0)


def _rwkv_seq_kernel(r_ref, w_ref, k_ref, v_ref, kk_ref, a_ref, s0_ref, y_ref, st_ref, s_scr, sb_scr, vk_scr):
    c = pl.program_id(1)
    t_len = r_ref.shape[1]
    npair = r_ref.shape[2]
    n = B_HEAD

    @pl.when(c == 0)
    def _():
        s_scr[...] = s0_ref[0]
        sb_scr[...] = s0_ref[0].astype(BF16)

    low = lax.broadcasted_iota(jnp.int32, (1, LANES), 1) < n
    eye = jnp.where((lax.broadcasted_iota(jnp.int32, (n, LANES), 1) & (n - 1))
                    == lax.broadcasted_iota(jnp.int32, (n, LANES), 0), 1.0, 0.0).astype(BF16)
    eye3 = jnp.concatenate([eye, eye, eye], axis=1)

    def per_head_rows(x, rows):
        x0 = jnp.where(low, x, 0.0).astype(BF16)
        x1 = jnp.where(low, 0.0, x).astype(BF16)
        return jnp.concatenate([jnp.broadcast_to(x0, (rows, LANES)), jnp.broadcast_to(x1, (rows, LANES))], axis=0)

    def outer(t, carry):
        v_all, k_all = v_ref[0, t], k_ref[0, t]
        for p in range(npair):
            vrow = v_all[p:p + 1]
            hi = vrow.astype(BF16).astype(F32)
            mid = (vrow - hi).astype(BF16).astype(F32)
            lo = vrow - hi - mid
            wmat = jnp.concatenate([per_head_rows(hi, n), per_head_rows(mid, n), per_head_rows(lo, n)], axis=1)
            vk_scr[p, t] = _dot_nt(eye3, wmat) * k_all[p:p + 1]
        return carry

    lax.fori_loop(0, t_len, outer, 0)

    def step(t, carry):
        kk_all, a_all, w_all, r_all = kk_ref[0, t], a_ref[0, t], w_ref[0, t], r_ref[0, t]
        for p in range(npair):
            kk = kk_all[p:p + 1]
            sa = _dot_nt(sb_scr[p], per_head_rows(-kk, n))
            s = s_scr[p] * w_all[p:p + 1] + sa * (kk * a_all[p:p + 1]) + vk_scr[p, t]
            sb = s.astype(BF16)
            s_scr[p] = s
            sb_scr[p] = sb
            yy = _dot_nt(per_head_rows(r_all[p:p + 1], 8), jnp.concatenate([sb, sb], axis=0))
            y_ref[0, t, p:p + 1, :] = jnp.where(low, yy[0:1], yy[8:9])
        return carry

    lax.fori_loop(0, t_len, step, 0)

    @pl.when(c == pl.num_programs(1) - 1)
    def _():
        st_ref[0] = s_scr[...]


def rwkv_seq(r, w, k, v, kk, a, s0, t_blk=RWKV_SEQ_T):
    b, l, d = r.shape
    npair = d // LANES
    t_blk = min(t_blk, l)
    assert l % t_blk == 0
    seq = pl.BlockSpec((1, t_blk, npair, LANES), lambda bi, ci: (bi, ci, 0, 0))
    sspec = pl.BlockSpec((1, npair, B_HEAD, LANES), lambda bi, ci: (bi, 0, 0, 0))
    y, s_new = pl.pallas_call(
        _rwkv_seq_kernel, name="rwkv_seq", grid=(b, l // t_blk),
        in_specs=[seq] * 6 + [sspec],
        out_specs=[seq, sspec],
        out_shape=[jax.ShapeDtypeStruct((b, l, npair, LANES), F32), jax.ShapeDtypeStruct(s0.shape, F32)],
        scratch_shapes=[pltpu.VMEM((npair, B_HEAD, LANES), F32), pltpu.VMEM((npair, B_HEAD, LANES), BF16),
                        pltpu.VMEM((npair, t_blk, B_HEAD, LANES), F32)],
        compiler_params=_cparams(("parallel", "arbitrary")))(
            *[z.reshape(b, l, npair, LANES) for z in (r, w, k, v, kk, a)], s0)
    return y.reshape(b, l, d), s_new


def _pair_states(s):
    b = s.shape[0]
    s = s.reshape(b, B_HEADS // 2, 2, B_HEAD, B_HEAD)
    return jnp.swapaxes(s, 2, 3).reshape(b, B_HEADS // 2, B_HEAD, 2 * B_HEAD)


def _unpair_states(sp):
    b = sp.shape[0]
    s = sp.reshape(b, B_HEADS // 2, B_HEAD, 2, B_HEAD)
    return jnp.swapaxes(s, 2, 3).reshape(b, B_HEADS, B_HEAD, B_HEAD)


def _pack_states(s):
    b = s.shape[0]
    s = s.reshape(b, B_HEADS // 2, 2, B_HEAD, B_HEAD)
    z = jnp.zeros_like(s[:, :, 0])
    top = jnp.concatenate([s[:, :, 0], z], axis=-1)
    bot = jnp.concatenate([z, s[:, :, 1]], axis=-1)
    return jnp.concatenate([top, bot], axis=-2)


def _unpack_states(sp):
    b = sp.shape[0]
    s0 = sp[:, :, :B_HEAD, :B_HEAD]
    s1 = sp[:, :, B_HEAD:, B_HEAD:]
    return jnp.stack([s0, s1], axis=2).reshape(b, B_HEADS, B_HEAD, B_HEAD)


def rwkv7_mix(pb, shift_prev, wkv_prev, lp):
    b, t, _ = pb.shape
    prev = jnp.concatenate([shift_prev[:, None], pb[:, :-1]], axis=1)
    xm = pb + (prev - pb) * lp['mu']
    splits = [B_DIM, 2 * B_DIM, 3 * B_DIM, 3 * B_DIM + B_DECAY_RANK, 3 * B_DIM + B_DECAY_RANK + B_ICLR_RANK]
    r, k, v, wl, al, gl = jnp.split(xm, splits, axis=-1)
    logw = -jax.nn.softplus(-(lp['w0'] + jnp.tanh(wl) @ lp['w2'])) - 0.5
    lw = -jnp.exp(logw)
    a = jax.nn.sigmoid(lp['a0'] + al @ lp['a2'])
    g = jax.nn.sigmoid(gl) @ lp['g2']
    heads = lambda z: z.reshape(b, t, B_HEADS, B_HEAD)
    kkh = heads(k * lp['k_k'])
    kkh = kkh * lax.rsqrt(jnp.sum(kkh * kkh, axis=-1, keepdims=True) + 1e-12)
    kk = kkh.reshape(b, t, B_DIM)
    k2 = k * (1.0 + (a - 1.0) * lp['k_a'])
    t_head = max(t - RWKV_TAIL, 0) // RWKV_T * RWKV_T
    seqs = [r, lw, k2, v, kk, a]
    state = wkv_prev
    ys = []
    if t_head:
        y_head, s_bd = rwkv_scan(*[z[:, :t_head] for z in seqs], _pack_states(state))
        state = _unpack_states(s_bd)
        ys.append(y_head)
    seqs[1] = jnp.exp(lw)
    y_tail, s_pair = rwkv_seq(*[z[:, t_head:] for z in seqs], _pair_states(state))
    ys.append(y_tail)
    y = heads(jnp.concatenate(ys, axis=1))
    mu = jnp.mean(y, axis=-1, keepdims=True)
    var = jnp.mean(jnp.square(y - mu), axis=-1, keepdims=True)
    yn = ((y - mu) * lax.rsqrt(var + B_GN_EPS)).reshape(b, t, B_DIM) * lp['ln_g'] + lp['ln_b']
    bonus = jnp.sum(heads(r) * heads(k2) * lp['r_k'], axis=-1, keepdims=True) * heads(v)
    o = (yn + bonus.reshape(b, t, B_DIM)) * g
    return o, pb[:, -1], _unpair_states(s_pair)


def causal_dwconv(u, prev, w, bf16_taps):
    width = w.shape[0]
    t = u.shape[1]
    full = jnp.concatenate([prev, u], axis=1)
    src = lax.reduce_precision(full, exponent_bits=8, mantissa_bits=7) if bf16_taps else full
    y = sum(src[:, j:j + t] * w[j] for j in range(width))
    return y, full[:, full.shape[1] - (width - 1):]


def layer_norm(x, g, b):
    mu = jnp.mean(x, axis=-1, keepdims=True)
    var = jnp.mean(jnp.square(x - mu), axis=-1, keepdims=True)
    return (x - mu) * lax.rsqrt(var + NORM_EPS) * g + b


def _moe_up_kernel(be_ref, nb_ref, x_ref, w_ref, b_ref, o_ref):
    i = pl.program_id(0)

    @pl.when(i < nb_ref[0])
    def _():
        ff = o_ref.shape[1]
        tf = min(ff, MOE_FF_TILE)
        x = x_ref[...]
        for f in range(ff // tf):
            gcol = slice(f * tf, (f + 1) * tf)
            ucol = slice(ff + f * tf, ff + (f + 1) * tf)
            gate = _dot(x, w_ref[0, :, gcol]) + b_ref[0, :, gcol]
            up = _dot(x, w_ref[0, :, ucol]) + b_ref[0, :, ucol]
            gate = jnp.minimum(gate, SWIGLU_LIMIT)
            up = jnp.clip(up, -SWIGLU_LIMIT, SWIGLU_LIMIT)
            act = gate * jax.nn.sigmoid(SWIGLU_ALPHA * gate) * (up + 1.0)
            o_ref[:, gcol] = act.astype(o_ref.dtype)

    @pl.when(i >= nb_ref[0])
    def _():
        o_ref[...] = jnp.zeros(o_ref.shape, o_ref.dtype)


def _moe_down_kernel(be_ref, nb_ref, a_ref, w_ref, b_ref, p_ref, o_ref):
    i = pl.program_id(0)

    @pl.when(i < nb_ref[0])
    def _():
        o_ref[...] = (_dot(a_ref[...], w_ref[0]) + b_ref[0]) * p_ref[...]

    @pl.when(i >= nb_ref[0])
    def _():
        o_ref[...] = jnp.zeros(o_ref.shape, o_ref.dtype)


def moe_experts(xb, row_p, block_e, n_used, w_gu, b_gu, w_down, b_down):
    rows, d = xb.shape
    nblk = rows // MOE_BLK
    ff = w_down.shape[1]
    cp = pltpu.CompilerParams(dimension_semantics=("arbitrary",), vmem_limit_bytes=MOE_VMEM_LIMIT)
    act = pl.pallas_call(
        _moe_up_kernel,
        grid_spec=pltpu.PrefetchScalarGridSpec(
            num_scalar_prefetch=2, grid=(nblk,),
            in_specs=[pl.BlockSpec((MOE_BLK, d), lambda i, be, nb: (i, 0)),
                      pl.BlockSpec((1, d, 2 * ff), lambda i, be, nb: (be[i], 0, 0)),
                      pl.BlockSpec((1, 1, 2 * ff), lambda i, be, nb: (be[i], 0, 0))],
            out_specs=pl.BlockSpec((MOE_BLK, ff), lambda i, be, nb: (i, 0))),
        out_shape=jax.ShapeDtypeStruct((rows, ff), BF16),
        compiler_params=cp)(block_e, n_used, xb, w_gu, b_gu)
    return pl.pallas_call(
        _moe_down_kernel,
        grid_spec=pltpu.PrefetchScalarGridSpec(
            num_scalar_prefetch=2, grid=(nblk,),
            in_specs=[pl.BlockSpec((MOE_BLK, ff), lambda i, be, nb: (i, 0)),
                      pl.BlockSpec((1, ff, d), lambda i, be, nb: (be[i], 0, 0)),
                      pl.BlockSpec((1, 1, d), lambda i, be, nb: (be[i], 0, 0)),
                      pl.BlockSpec((MOE_BLK, 1), lambda i, be, nb: (i, 0))],
            out_specs=pl.BlockSpec((MOE_BLK, d), lambda i, be, nb: (i, 0))),
        out_shape=jax.ShapeDtypeStruct((rows, d), F32),
        compiler_params=cp)(block_e, n_used, act, w_down, b_down, row_p)


def moe_ffn(h, logits, w_gu, b_gu, w_down, b_down):
    n, d = h.shape
    top_val, top_idx = lax.top_k(logits, TOP_K)
    probs = jax.nn.softmax(top_val, axis=-1)
    flat_e = top_idx.reshape(-1).astype(jnp.int32)
    order = jnp.argsort(flat_e, stable=True).astype(jnp.int32)
    e_sorted = flat_e[order]
    counts = jnp.bincount(flat_e, length=N_EXPERTS).astype(jnp.int32)
    padded = (counts + MOE_BLK - 1) // MOE_BLK * MOE_BLK
    pad_end = jnp.cumsum(padded)
    pad_start = pad_end - padded
    start = jnp.cumsum(counts) - counts
    dest = pad_start[e_sorted] + jnp.arange(n * TOP_K, dtype=jnp.int32) - start[e_sorted]
    nblk = -(-(n * TOP_K + N_EXPERTS * (MOE_BLK - 1)) // MOE_BLK)
    rows = nblk * MOE_BLK
    row_tok = jnp.full((rows,), n, jnp.int32).at[dest].set(order // TOP_K)
    row_p = jnp.zeros((rows,), F32).at[dest].set(probs.reshape(-1)[order])
    block_e = jnp.minimum(jnp.searchsorted(pad_end, jnp.arange(nblk, dtype=jnp.int32) * MOE_BLK, side='right'),
                          N_EXPERTS - 1).astype(jnp.int32)
    n_used = (pad_end[-1] // MOE_BLK).astype(jnp.int32).reshape(1)
    xb = jnp.concatenate([h, jnp.zeros((1, d), h.dtype)], axis=0)[row_tok]
    yb = moe_experts(xb, row_p.reshape(rows, 1), block_e, n_used,
                     w_gu, b_gu.reshape(N_EXPERTS, 1, -1), w_down, b_down.reshape(N_EXPERTS, 1, -1))
    pos = jnp.zeros((n * TOP_K,), jnp.int32).at[order].set(dest)
    return jnp.sum(yb[pos].reshape(n, TOP_K, d), axis=1)


def _split_groups(z):
    return z[:N_PROMPT].reshape(1, SEQ, -1), z[N_PROMPT:].reshape(DEC_BATCH, DEC_SEQ, -1)


def _join_groups(zp, zs):
    return jnp.concatenate([zp.reshape(N_PROMPT, -1), zs.reshape(N_SAMPLE, -1)], axis=0)


def layer(x, mod_g, lp, st, attn_tabs):
    d = D_MODEL
    h = norm_mod(x, lp['norm1_g'], mod_g, 0, 1)
    qkv = matmul(h, lp['w_qkv'], tm=768, tn=512)
    pb = matmul(h, lp['w_pb'], tm=768, tn=1280)
    bch = matmul(h, lp['w_bch'], tm=768, tn=512)
    pd = matmul(h, lp['w_pd'], tm=768, tn=512)
    gates = matmul(h, lp['w_gl'], tm=768, tn=512, act="sigmoid")

    (bias_p, sink_p), (bias_s, sink_s) = attn_tabs
    o_a = jnp.concatenate([
        attn_prompt(qkv, bias_p, sink_p, N_PROMPT),
        attn_sample(qkv, st['k'].reshape(DEC_BATCH, WINDOW, A_KV), st['v'].reshape(DEC_BATCH, WINDOW, A_KV),
                    bias_s, sink_s, N_PROMPT)], axis=0)
    kp, ks = _split_groups(qkv[:, OFF_K:OFF_V])
    vp, vs = _split_groups(qkv[:, OFF_V:OFF_PB])
    new_kv_p = (kp[:, SEQ - WINDOW:].reshape(1, WINDOW, A_KV_HEADS, A_HEAD_DIM),
                vp[:, SEQ - WINDOW:].reshape(1, WINDOW, A_KV_HEADS, A_HEAD_DIM))
    new_kv_s = (ks.reshape(DEC_BATCH, DEC_SEQ, A_KV_HEADS, A_HEAD_DIM),
                vs.reshape(DEC_BATCH, DEC_SEQ, A_KV_HEADS, A_HEAD_DIM))

    pb_p, pb_s = _split_groups(pb)
    ob_p, shift_p, wkv_p = rwkv7_mix(pb_p, jnp.zeros((1, B_PROJ), F32),
                                     jnp.zeros((1, B_HEADS, B_HEAD, B_HEAD), F32), lp)
    ob_s, shift_s, wkv_s = rwkv7_mix(pb_s, st['shift'], st['wkv'], lp)
    o_b = _join_groups(ob_p, ob_s)

    bg, cg, hc = bch[:, :d], bch[:, d:2 * d], bch[:, 2 * d:]
    ucp, ucs = _split_groups(cg * hc)
    ycp, sconv_p = causal_dwconv(ucp, jnp.zeros((1, C_WIDTH - 1, d), F32), lp['sconv_w'], True)
    ycs, sconv_s = causal_dwconv(ucs, st['sconv'], lp['sconv_w'], False)
    o_c = bg * _join_groups(ycp, ycs)

    udp, uds = _split_groups(pd[:, :d] * jax.nn.sigmoid(pd[:, d:]))
    ydp, cconv_p = causal_dwconv(udp, jnp.zeros((1, D_WIDTH - 1, d), F32), lp['cconv_w'], True)
    yds, cconv_s = causal_dwconv(uds, st['cconv'], lp['cconv_w'], False)
    o_d = jax.nn.silu(layer_norm(_join_groups(ydp, yds) + lp['cconv_b'], lp['cnorm_g'], lp['cnorm_b']))

    branches = jnp.stack([o_a.astype(BF16), o_b.astype(BF16), o_c.astype(BF16), o_d.astype(BF16)], axis=0)
    merged = merge_branches(branches, lp['w_branch'], gates)
    x = matmul_resid(merged, lp['w_out'], x, mod_g, 2)

    h2, logits = norm_mod(x, lp['norm2_g'], mod_g, 3, 4, router=(lp['router_w'], lp['router_b']))
    ffn = moe_ffn(h2, logits, lp['w_gu'], lp['b_gu'], lp['w_down'], lp['b_down'])
    x = resid_gate(x, ffn, mod_g, 5)
    new_p = (new_kv_p[0], new_kv_p[1], wkv_p, shift_p, sconv_p, cconv_p)
    new_s = (new_kv_s[0], new_kv_s[1], wkv_s, shift_s, sconv_s, cconv_s)
    return x, new_p, new_s


def kernel(x_prompt, x_sample, c_prompt, c_sample, cache_k, cache_v, state_wkv, state_shift, state_sconv, state_cconv, w_mod, b_mod, norm1_g, norm2_g, w_in, attn_sinks, rel_bias_table, rwkv_mu, rwkv_w0, rwkv_w2, rwkv_a0, rwkv_a2, rwkv_g2, rwkv_k_k, rwkv_k_a, rwkv_r_k, rwkv_ln_g, rwkv_ln_b, sconv_w, cconv_w, cconv_b, cnorm_g, cnorm_b, w_branch, w_out, router_w, router_b, expert_w_gu, expert_b_gu, expert_w_down, expert_b_down, final_g):
    d = D_MODEL
    x = jnp.concatenate([x_prompt.reshape(N_PROMPT, d), x_sample.reshape(N_SAMPLE, d)], axis=0)
    c_all = jnp.concatenate([c_prompt, c_sample], axis=0)
    c_act = jnp.pad(jax.nn.silu(c_all), ((0, 16 - c_all.shape[0]), (0, 0))).astype(BF16)

    blk_pos = jnp.arange(CHUNK, dtype=jnp.int32)
    qpos_p = WINDOW + blk_pos
    kpos_p = jnp.arange(WINDOW + CHUNK, dtype=jnp.int32)
    qpos_s = PAST_LEN + jnp.arange(DEC_SEQ, dtype=jnp.int32)
    kpos_s = PAST_LEN - WINDOW + jnp.arange(WINDOW + DEC_SEQ, dtype=jnp.int32)

    st_p, st_s = [], []
    for l in range(DEPTH):
        w_in_l = w_in[l]
        lp = dict(norm1_g=norm1_g[l], norm2_g=norm2_g[l],
                  w_qkv=w_in_l[:, OFF_Q:OFF_PB].astype(BF16), w_pb=w_in_l[:, OFF_PB:OFF_BG].astype(BF16),
                  w_bch=w_in_l[:, OFF_BG:OFF_PD].astype(BF16), w_pd=w_in_l[:, OFF_PD:OFF_GL].astype(BF16),
                  w_gl=w_in_l[:, OFF_GL:].astype(BF16),
                  mu=rwkv_mu[l], w0=rwkv_w0[l], w2=rwkv_w2[l], a0=rwkv_a0[l], a2=rwkv_a2[l],
                  g2=rwkv_g2[l], k_k=rwkv_k_k[l], k_a=rwkv_k_a[l], r_k=rwkv_r_k[l], ln_g=rwkv_ln_g[l],
                  ln_b=rwkv_ln_b[l], sconv_w=sconv_w[l], cconv_w=cconv_w[l], cconv_b=cconv_b[l],
                  cnorm_g=cnorm_g[l], cnorm_b=cnorm_b[l], w_branch=w_branch[l].astype(BF16),
                  w_out=w_out[l].astype(BF16), router_w=router_w[l], router_b=router_b[l],
                  w_gu=expert_w_gu[l].astype(BF16), b_gu=expert_b_gu[l],
                  w_down=expert_w_down[l].astype(BF16), b_down=expert_b_down[l])
        mod = matmul(c_act, w_mod[l].astype(BF16), tm=16, tn=512)[:c_all.shape[0]] + b_mod[l]
        mod_g = jnp.concatenate([jnp.broadcast_to(mod[:1], (N_PROMPT // MOD_GROUP, 6 * d)),
                                 jnp.repeat(mod[1:], DEC_SEQ // MOD_GROUP, axis=0)], axis=0)
        tabs = (_attn_tables(qpos_p, kpos_p, rel_bias_table, attn_sinks[l], False),
                _attn_tables(qpos_s, kpos_s, rel_bias_table, attn_sinks[l], True))
        st = dict(k=cache_k[l], v=cache_v[l], shift=state_shift[l], wkv=state_wkv[l],
                  sconv=state_sconv[l], cconv=state_cconv[l])
        x, sp, ss = layer(x, mod_g, lp, st, tabs)
        st_p.append(sp)
        st_s.append(ss)
    y = final_norm(x, final_g)
    y_prompt = y[:N_PROMPT].reshape(1, SEQ, d)
    y_sample = y[N_PROMPT:].reshape(DEC_BATCH, DEC_SEQ, d)
    return (y_prompt, y_sample,
            jnp.stack([s[0] for s in st_p]), jnp.stack([s[1] for s in st_p]), jnp.stack([s[2] for s in st_p]),
            jnp.stack([s[3] for s in st_p]), jnp.stack([s[4] for s in st_p]), jnp.stack([s[5] for s in st_p]),
            jnp.stack([s[0] for s in st_s]), jnp.stack([s[1] for s in st_s]), jnp.stack([s[2] for s in st_s]),
            jnp.stack([s[3] for s in st_s]), jnp.stack([s[4] for s in st_s]), jnp.stack([s[5] for s in st_s]))
```

```python
import functools
import math

import jax
import jax.numpy as jnp
from jax import lax
from jax.experimental import pallas as pl
from jax.experimental.pallas import tpu as pltpu

F32 = jnp.float32
BF16 = jnp.bfloat16

D_MODEL = 2048
SEQ = 8192
DEPTH = 2
DEC_BATCH = 8
DEC_SEQ = 32
PAST_LEN = 1024
CHUNK = 64
NORM_EPS = 1e-5
NEG_INF = -1e30

A_HEAD_DIM = 64
A_HEADS = 32
A_KV_HEADS = 8
A_GROUP = 4
WINDOW = 128
WIN_CHUNKS = 2
NUM_BUCKETS = 32
MAX_DISTANCE = 128
A_Q = A_HEADS * A_HEAD_DIM
A_KV = A_KV_HEADS * A_HEAD_DIM

B_HEAD = 64
B_HEADS = 32
B_DIM = 2048
B_DECAY_RANK = 64
B_ICLR_RANK = 64
B_GATE_RANK = 128
B_GN_EPS = 64e-5
B_PROJ = 3 * B_DIM + B_DECAY_RANK + B_ICLR_RANK + B_GATE_RANK

C_WIDTH = 3
D_WIDTH = 31
N_BRANCH = 4

N_EXPERTS = 32
TOP_K = 4
D_FF = D_MODEL
SWIGLU_LIMIT = 7.0
SWIGLU_ALPHA = 1.702

OFF_Q = 0
OFF_K = OFF_Q + A_Q
OFF_V = OFF_K + A_KV
OFF_PB = OFF_V + A_KV
OFF_BG = OFF_PB + B_PROJ
OFF_CG = OFF_BG + D_MODEL
OFF_HC = OFF_CG + D_MODEL
OFF_PD = OFF_HC + D_MODEL
OFF_GL = OFF_PD + 2 * D_MODEL
IN_TOTAL = OFF_GL + N_BRANCH * D_MODEL

N_PROMPT = SEQ
N_SAMPLE = DEC_BATCH * DEC_SEQ
N_TOK = N_PROMPT + N_SAMPLE
MOD_GROUP = 32
N_MOD_GROUPS = N_TOK // MOD_GROUP

LANES = 128
RWKV_T = 64
RWKV_PAIRS = 16
RWKV_SEQ_T = 16
RWKV_TAIL = 256
CONV_ROWS = 256
CONV_STRIP = 512
CONV_C_HALO = 8
CONV_D_HALO = 32
MOE_BLK = 256
MOE_FF_TILE = 512
VMEM_LIMIT = 48 * 1024 * 1024
MOE_VMEM_LIMIT = 56 * 1024 * 1024


def _cparams(sem):
    return pltpu.CompilerParams(dimension_semantics=sem, vmem_limit_bytes=VMEM_LIMIT)


def _dot(a, b):
    return jnp.dot(a, b, preferred_element_type=F32)


def _dot_nt(a, b):
    return lax.dot_general(a, b, (((1,), (1,)), ((), ())), preferred_element_type=F32)


def _mm_kernel(x_ref, w_ref, o_ref, *, act):
    y = _dot(x_ref[...], w_ref[...])
    if act == "sigmoid":
        y = jax.nn.sigmoid(y)
    o_ref[...] = y.astype(o_ref.dtype)


def matmul(x, w, *, name, tm, tn, out_dtype=F32, act=None):
    m, k = x.shape
    n = w.shape[1]
    assert m % tm == 0 and n % tn == 0
    return pl.pallas_call(
        functools.partial(_mm_kernel, act=act), name=name,
        grid=(m // tm, n // tn),
        in_specs=[pl.BlockSpec((tm, k), lambda i, j: (i, 0)),
                  pl.BlockSpec((k, tn), lambda i, j: (0, j))],
        out_specs=pl.BlockSpec((tm, tn), lambda i, j: (i, j)),
        out_shape=jax.ShapeDtypeStruct((m, n), out_dtype),
        compiler_params=_cparams(("parallel", "parallel")),
    )(x, w)


def _group_rows(gi):
    return slice(gi * MOD_GROUP, (gi + 1) * MOD_GROUP)


def _norm_mod_rows(x_ref, g_ref, sh_ref, sc_ref, gi):
    x = x_ref[_group_rows(gi), :]
    y = x * lax.rsqrt(jnp.mean(x * x, axis=-1, keepdims=True) + NORM_EPS) * g_ref[...]
    return y * (1.0 + sc_ref[gi:gi + 1, :]) + sh_ref[gi:gi + 1, :]


def _norm_mod_kernel(x_ref, g_ref, sh_ref, sc_ref, h_ref):
    for gi in range(x_ref.shape[0] // MOD_GROUP):
        h_ref[_group_rows(gi), :] = _norm_mod_rows(x_ref, g_ref, sh_ref, sc_ref, gi).astype(h_ref.dtype)


def _norm_mod_router_kernel(x_ref, g_ref, sh_ref, sc_ref, rw_ref, rb_ref, h_ref, lg_ref):
    w = rw_ref[...].astype(BF16)
    for gi in range(x_ref.shape[0] // MOD_GROUP):
        h = _norm_mod_rows(x_ref, g_ref, sh_ref, sc_ref, gi).astype(BF16)
        h_ref[_group_rows(gi), :] = h
        lg_ref[_group_rows(gi), :] = _dot(h, w) + rb_ref[...]


def norm_mod(x, g, mod_g, shift_idx, scale_idx, router=None, tm=256):
    m, d = x.shape
    gm = tm // MOD_GROUP
    in_specs = [pl.BlockSpec((tm, d), lambda i: (i, 0)),
                pl.BlockSpec((1, d), lambda i: (0, 0)),
                pl.BlockSpec((gm, d), lambda i: (i, shift_idx)),
                pl.BlockSpec((gm, d), lambda i: (i, scale_idx))]
    args = [x, g.reshape(1, d), mod_g, mod_g]
    if router is None:
        return pl.pallas_call(
            _norm_mod_kernel, name="norm_mod", grid=(m // tm,), in_specs=in_specs,
            out_specs=pl.BlockSpec((tm, d), lambda i: (i, 0)),
            out_shape=jax.ShapeDtypeStruct((m, d), BF16),
            compiler_params=_cparams(("parallel",)))(*args)
    rw, rb = router
    ne = rw.shape[1]
    in_specs += [pl.BlockSpec((d, ne), lambda i: (0, 0)), pl.BlockSpec((1, ne), lambda i: (0, 0))]
    args += [rw, rb.reshape(1, ne)]
    return pl.pallas_call(
        _norm_mod_router_kernel, name="norm_router", grid=(m // tm,), in_specs=in_specs,
        out_specs=[pl.BlockSpec((tm, d), lambda i: (i, 0)), pl.BlockSpec((tm, ne), lambda i: (i, 0))],
        out_shape=[jax.ShapeDtypeStruct((m, d), BF16), jax.ShapeDtypeStruct((m, ne), F32)],
        compiler_params=_cparams(("parallel",)))(*args)


def _final_norm_kernel(x_ref, g_ref, o_ref):
    x = x_ref[...]
    o_ref[...] = x * lax.rsqrt(jnp.mean(x * x, axis=-1, keepdims=True) + NORM_EPS) * g_ref[...]


def final_norm(x, g, tm=256):
    m, d = x.shape
    return pl.pallas_call(
        _final_norm_kernel, name="final_norm", grid=(m // tm,),
        in_specs=[pl.BlockSpec((tm, d), lambda i: (i, 0)), pl.BlockSpec((1, d), lambda i: (0, 0))],
        out_specs=pl.BlockSpec((tm, d), lambda i: (i, 0)),
        out_shape=jax.ShapeDtypeStruct((m, d), F32),
        compiler_params=_cparams(("parallel",)))(x, g.reshape(1, d))


def _mm_resid_kernel(a_ref, w_ref, x_ref, gate_ref, o_ref):
    y = _dot(a_ref[...], w_ref[...])
    for gi in range(y.shape[0] // MOD_GROUP):
        rows = _group_rows(gi)
        o_ref[rows, :] = x_ref[rows, :] + gate_ref[gi:gi + 1, :] * y[rows]


def matmul_resid(a, w, x, mod_g, gate_idx, *, tm=768, tn=512):
    m, k = a.shape
    n = w.shape[1]
    gm = tm // MOD_GROUP
    nj = n // tn
    return pl.pallas_call(
        _mm_resid_kernel, name="out_proj", grid=(m // tm, nj),
        in_specs=[pl.BlockSpec((tm, k), lambda i, j: (i, 0)),
                  pl.BlockSpec((k, tn), lambda i, j: (0, j)),
                  pl.BlockSpec((tm, tn), lambda i, j: (i, j)),
                  pl.BlockSpec((gm, tn), lambda i, j: (i, gate_idx * nj + j))],
        out_specs=pl.BlockSpec((tm, tn), lambda i, j: (i, j)),
        out_shape=jax.ShapeDtypeStruct((m, n), F32),
        compiler_params=_cparams(("parallel", "parallel")))(a, w, x, mod_g)


def _resid_kernel(x_ref, f_ref, gate_ref, o_ref):
    for gi in range(x_ref.shape[0] // MOD_GROUP):
        rows = _group_rows(gi)
        o_ref[rows, :] = x_ref[rows, :] + gate_ref[gi:gi + 1, :] * f_ref[rows, :]


def resid_gate(x, f, mod_g, gate_idx, tm=256):
    m, d = x.shape
    gm = tm // MOD_GROUP
    return pl.pallas_call(
        _resid_kernel, name="ffn_resid", grid=(m // tm,),
        in_specs=[pl.BlockSpec((tm, d), lambda i: (i, 0)), pl.BlockSpec((tm, d), lambda i: (i, 0)),
                  pl.BlockSpec((gm, d), lambda i: (i, gate_idx))],
        out_specs=pl.BlockSpec((tm, d), lambda i: (i, 0)),
        out_shape=jax.ShapeDtypeStruct((m, d), F32),
        compiler_params=_cparams(("parallel",)))(x, f, mod_g)


def _merge_kernel(ba_ref, bb_ref, bc_ref, bd_ref, w_ref, ga_ref, gb_ref, gc_ref, gd_ref, o_ref):
    acc = None
    for n, (b_ref, g_ref) in enumerate(((ba_ref, ga_ref), (bb_ref, gb_ref), (bc_ref, gc_ref), (bd_ref, gd_ref))):
        part = g_ref[...] * _dot(b_ref[...], w_ref[0, n])
        acc = part if acc is None else acc + part
    o_ref[...] = acc.astype(o_ref.dtype)


def merge_branches(branches, w_branch, layer, gates, *, tm=768, tn=256):
    m, k = branches[0].shape
    d = w_branch.shape[3]
    nj = d // tn
    bspec = pl.BlockSpec((tm, k), lambda i, j: (i, 0))
    gspec = lambda n: pl.BlockSpec((tm, tn), lambda i, j: (i, n * nj + j))
    return pl.pallas_call(
        _merge_kernel, name="merge", grid=(m // tm, nj),
        in_specs=[bspec] * N_BRANCH + [pl.BlockSpec((1, N_BRANCH, k, tn), lambda i, j: (layer, 0, 0, j))]
                 + [gspec(n) for n in range(N_BRANCH)],
        out_specs=pl.BlockSpec((tm, tn), lambda i, j: (i, j)),
        out_shape=jax.ShapeDtypeStruct((m, d), BF16),
        compiler_params=_cparams(("parallel", "parallel")))(*branches, w_branch, gates, gates, gates, gates)


def _attn_core(q, k, v, bias_ref, sink_ref, o_ref, key_ok):
    sq = q.shape[0]
    lane = lax.broadcasted_iota(jnp.int32, (1, LANES), 1)
    low = lane < A_HEAD_DIM
    q = q * (A_HEAD_DIM ** -0.5)
    for kvp in range(A_KV_HEADS // 2):
        kp = k[:, kvp * LANES:(kvp + 1) * LANES]
        vp = v[:, kvp * LANES:(kvp + 1) * LANES]
        kp_sw = pltpu.roll(kp, A_HEAD_DIM, 1)
        vp_sw = pltpu.roll(vp, A_HEAD_DIM, 1)
        for half in range(2):
            g = 2 * kvp + half
            if half == 0:
                k2 = jnp.where(low, kp, kp_sw)
                v2 = jnp.where(low, vp, vp_sw)
            else:
                k2 = jnp.where(low, kp_sw, kp)
                v2 = jnp.where(low, vp_sw, vp)
            parts = []
            for j in range(2):
                qp = q[:, (2 * g + j) * LANES:(2 * g + j + 1) * LANES]
                parts.append(jnp.where(low, qp, 0.0))
                parts.append(jnp.where(low, 0.0, qp))
            lhs = jnp.concatenate(parts, axis=0).astype(BF16)
            s = _dot_nt(lhs, k2.astype(BF16)) + bias_ref[g]
            if key_ok is not None:
                s = jnp.where(key_ok, s, NEG_INF)
            sink = sink_ref[g]
            m = jnp.maximum(jnp.max(s, axis=-1, keepdims=True), sink)
            p = jnp.exp(s - m)
            p = p / (jnp.sum(p, axis=-1, keepdims=True) + jnp.exp(sink - m))
            o2 = _dot(p.astype(BF16), v2.astype(BF16))
            for j in range(2):
                oj = jnp.where(low, o2[(2 * j) * sq:(2 * j + 1) * sq], o2[(2 * j + 1) * sq:(2 * j + 2) * sq])
                o_ref[:, (2 * g + j) * LANES:(2 * g + j + 1) * LANES] = oj.astype(o_ref.dtype)


def _attn_prompt_kernel(q_ref, k0_ref, k1_ref, k2_ref, v0_ref, v1_ref, v2_ref, bias_ref, sink_ref, o_ref):
    c = pl.program_id(0)
    k = jnp.concatenate([k0_ref[...], k1_ref[...], k2_ref[...]], axis=0)
    v = jnp.concatenate([v0_ref[...], v1_ref[...], v2_ref[...]], axis=0)
    sk = k.shape[0]
    k_pos = (c - WIN_CHUNKS) * CHUNK + lax.broadcasted_iota(jnp.int32, (1, sk), 1)
    _attn_core(q_ref[...], k, v, bias_ref, sink_ref, o_ref, k_pos >= 0)


_QKV_KCOL = A_Q // A_KV
_QKV_VCOL = _QKV_KCOL + 1


def attn_prompt(qkv, bias, sink_col, n_prompt):
    nc = n_prompt // CHUNK
    kspec = lambda d, col: pl.BlockSpec((CHUNK, A_KV), lambda c: (jnp.maximum(c - d, 0), col))
    return pl.pallas_call(
        _attn_prompt_kernel, name="attn_prompt", grid=(nc,),
        in_specs=[pl.BlockSpec((CHUNK, A_Q), lambda c: (c, 0)),
                  kspec(2, _QKV_KCOL), kspec(1, _QKV_KCOL), kspec(0, _QKV_KCOL),
                  kspec(2, _QKV_VCOL), kspec(1, _QKV_VCOL), kspec(0, _QKV_VCOL),
                  pl.BlockSpec(bias.shape, lambda c: (0, 0, 0)),
                  pl.BlockSpec(sink_col.shape, lambda c: (0, 0, 0))],
        out_specs=pl.BlockSpec((CHUNK, A_Q), lambda c: (c, 0)),
        out_shape=jax.ShapeDtypeStruct((n_prompt, A_Q), BF16),
        compiler_params=_cparams(("parallel",)))(qkv, qkv, qkv, qkv, qkv, qkv, qkv, bias, sink_col)


def _attn_sample_kernel(q_ref, kn_ref, vn_ref, kc_ref, vc_ref, bias_ref, sink_ref, o_ref):
    k = jnp.concatenate([kc_ref[0], kn_ref[...]], axis=0)
    v = jnp.concatenate([vc_ref[0], vn_ref[...]], axis=0)
    _attn_core(q_ref[...], k, v, bias_ref, sink_ref, o_ref, None)


def attn_sample(qkv, k_cache, v_cache, bias, sink_col, n_prompt):
    nb = k_cache.shape[0]
    base = n_prompt // DEC_SEQ
    return pl.pallas_call(
        _attn_sample_kernel, name="attn_sample", grid=(nb,),
        in_specs=[pl.BlockSpec((DEC_SEQ, A_Q), lambda b: (base + b, 0)),
                  pl.BlockSpec((DEC_SEQ, A_KV), lambda b: (base + b, _QKV_KCOL)),
                  pl.BlockSpec((DEC_SEQ, A_KV), lambda b: (base + b, _QKV_VCOL)),
                  pl.BlockSpec((1, WINDOW, A_KV), lambda b: (b, 0, 0)),
                  pl.BlockSpec((1, WINDOW, A_KV), lambda b: (b, 0, 0)),
                  pl.BlockSpec(bias.shape, lambda b: (0, 0, 0)),
                  pl.BlockSpec(sink_col.shape, lambda b: (0, 0, 0))],
        out_specs=pl.BlockSpec((DEC_SEQ, A_Q), lambda b: (b, 0)),
        out_shape=jax.ShapeDtypeStruct((nb * DEC_SEQ, A_Q), BF16),
        compiler_params=_cparams(("parallel",)))(qkv, qkv, qkv, k_cache, v_cache, bias, sink_col)


def _t5_bucket(rel):
    nb = NUM_BUCKETS // 2
    max_exact = nb // 2
    ret = jnp.where(rel > 0, nb, 0)
    n = jnp.abs(rel)
    nf = jnp.maximum(n, 1).astype(F32)
    large = max_exact + (jnp.log(nf / max_exact) / math.log(MAX_DISTANCE / max_exact) * (nb - max_exact)).astype(jnp.int32)
    large = jnp.minimum(large, nb - 1)
    return ret + jnp.where(n < max_exact, n, large)


def _attn_tables(q_pos, k_pos, table, sinks, mask_positions):
    sq, sk = q_pos.shape[0], k_pos.shape[0]
    b = table[_t5_bucket(k_pos[None, :] - q_pos[:, None])]
    b = jnp.transpose(b, (2, 0, 1)).astype(F32)
    if mask_positions:
        qc, kc = q_pos[:, None] // CHUNK, k_pos[None, :] // CHUNK
        vis = (k_pos[None, :] >= 0) & (kc <= qc) & (kc >= qc - WIN_CHUNKS)
        b = jnp.where(vis[None], b, NEG_INF)
    bias = b.reshape(A_KV_HEADS, A_GROUP * sq, sk)
    sink_col = jnp.broadcast_to(sinks.astype(F32).reshape(A_KV_HEADS, A_GROUP, 1, 1),
                                (A_KV_HEADS, A_GROUP, sq, 1)).reshape(A_KV_HEADS, A_GROUP * sq, 1)
    return bias, sink_col


def _rwkv_kernel(r_ref, lw_ref, k_ref, v_ref, kk_ref, a_ref, s0_ref, y_ref, st_ref, s_scr):
    c = pl.program_id(2)
    t = RWKV_T

    @pl.when(c == 0)
    def _():
        s_scr[...] = s0_ref[0]

    row = lax.broadcasted_iota(jnp.int32, (t, LANES), 0)
    lane = lax.broadcasted_iota(jnp.int32, (t, LANES), 1)
    low = lane < B_HEAD
    col = lane & (t - 1)
    strict = col < row
    incl = col <= row
    tri = jnp.where(lax.broadcasted_iota(jnp.int32, (t, t), 1) <= lax.broadcasted_iota(jnp.int32, (t, t), 0),
                    1.0, 0.0).astype(BF16)
    r128 = lax.broadcasted_iota(jnp.int32, (LANES, LANES), 0)
    c128 = lax.broadcasted_iota(jnp.int32, (LANES, LANES), 1)
    same_head = (r128 < B_HEAD) == (c128 < B_HEAD)

    def stack(x):
        return jnp.concatenate([jnp.where(low, x, 0.0), jnp.where(low, 0.0, x)], axis=0)

    pairs = range(RWKV_PAIRS)
    sls = [slice(p * LANES, (p + 1) * LANES) for p in pairs]
    n_steps = int(math.log2(t))

    def prepare(sl):
        lw = lw_ref[0, :, sl]
        hi = lw.astype(BF16)
        rem = lw - hi.astype(F32)
        mid = rem.astype(BF16)
        lo = (rem - mid.astype(F32)).astype(BF16)
        cs = _dot(tri, hi) + _dot(tri, mid) + _dot(tri, lo)
        cs_end = cs[t - 1:t, :]
        kk = kk_ref[0, :, sl]
        ka = kk * a_ref[0, :, sl]
        kx = k_ref[0, :, sl]
        g_inv = jnp.exp(-cs)
        g_end = jnp.exp(cs_end - cs)
        a_t = -kk * jnp.exp(cs - lw)
        r_t = r_ref[0, :, sl] * jnp.exp(cs)
        lhs = jnp.concatenate([a_t, r_t], axis=0).astype(BF16)
        rhs = jnp.concatenate([stack(ka * g_inv), stack(kx * g_inv)], axis=0).astype(BF16)
        tail = jnp.concatenate([ka * g_end, kx * g_end], axis=0).astype(BF16)
        return lhs, rhs, tail, jnp.exp(cs_end)

    prep = [prepare(sl) for sl in sls]
    vs = [v_ref[0, :, sl] for sl in sls]
    st_v = [stack(v).astype(BF16) for v in vs]
    ss = [s_scr[p] for p in pairs]
    d1 = [_dot_nt(prep[p][0], prep[p][1]) for p in pairs]
    d2 = [_dot_nt(prep[p][0], ss[p].astype(BF16)) for p in pairs]
    l_ak = [jnp.where(strict, d1[p][:t, 2 * t:], 0.0).astype(BF16) for p in pairs]
    pw = [jnp.where(strict, d1[p][:t, :2 * t], 0.0) for p in pairs]
    l_r = [jnp.concatenate([jnp.where(incl, d1[p][t:, :2 * t], 0.0), jnp.where(incl, d1[p][t:, 2 * t:], 0.0)],
                           axis=1).astype(BF16) for p in pairs]
    u = [d2[p][:t] + _dot(l_ak[p], st_v[p]) for p in pairs]
    for i in range(n_steps):
        u = [u[p] + _dot(pw[p].astype(BF16), stack(u[p]).astype(BF16)) for p in pairs]
        if i < n_steps - 1:
            pw = [_dot(pw[p].astype(BF16), stack(pw[p]).astype(BF16)) for p in pairs]
    y = [d2[p][t:] + _dot(l_r[p], jnp.concatenate([stack(u[p]).astype(BF16), st_v[p]], axis=0)) for p in pairs]
    upd = [_dot(jnp.concatenate([u[p], vs[p]], axis=0).T.astype(BF16), prep[p][2]) for p in pairs]
    for p in pairs:
        s_scr[p] = ss[p] * prep[p][3] + jnp.where(same_head, upd[p], 0.0)
        y_ref[0, :, sls[p]] = y[p]

    @pl.when(c == pl.num_programs(2) - 1)
    def _():
        st_ref[0] = s_scr[...]


def rwkv_scan(r, lw, k, v, kk, a, s0):
    b, l, d = r.shape
    npair = d // LANES
    ng = npair // RWKV_PAIRS
    w = RWKV_PAIRS * LANES
    seq = pl.BlockSpec((1, RWKV_T, w), lambda bi, gi, ci: (bi, ci, gi))
    sspec = pl.BlockSpec((1, RWKV_PAIRS, LANES, LANES), lambda bi, gi, ci: (bi, gi, 0, 0))
    return pl.pallas_call(
        _rwkv_kernel, name="rwkv_chunk", grid=(b, ng, l // RWKV_T),
        in_specs=[seq] * 6 + [sspec],
        out_specs=[seq, sspec],
        out_shape=[jax.ShapeDtypeStruct((b, l, d), F32), jax.ShapeDtypeStruct(s0.shape, F32)],
        scratch_shapes=[pltpu.VMEM((RWKV_PAIRS, LANES, LANES), F32)],
        compiler_params=_cparams(("parallel", "parallel", "arbitrary")))(r, lw, k, v, kk, a, s0)


def _rwkv_seq_kernel(r_ref, w_ref, k_ref, v_ref, kk_ref, a_ref, s0_ref, y_ref, st_ref, s_scr, sb_scr, vk_scr):
    c = pl.program_id(1)
    t_len = r_ref.shape[1]
    npair = r_ref.shape[2]
    n = B_HEAD

    @pl.when(c == 0)
    def _():
        s_scr[...] = s0_ref[0]
        sb_scr[...] = s0_ref[0].astype(BF16)

    low = lax.broadcasted_iota(jnp.int32, (1, LANES), 1) < n
    eye = jnp.where((lax.broadcasted_iota(jnp.int32, (n, LANES), 1) & (n - 1))
                    == lax.broadcasted_iota(jnp.int32, (n, LANES), 0), 1.0, 0.0).astype(BF16)
    eye3 = jnp.concatenate([eye, eye, eye], axis=1)

    def per_head_rows(x, rows):
        x0 = jnp.where(low, x, 0.0).astype(BF16)
        x1 = jnp.where(low, 0.0, x).astype(BF16)
        return jnp.concatenate([jnp.broadcast_to(x0, (rows, LANES)), jnp.broadcast_to(x1, (rows, LANES))], axis=0)

    def outer(t, carry):
        v_all, k_all = v_ref[0, t], k_ref[0, t]
        hi = v_all.astype(BF16).astype(F32)
        mid = (v_all - hi).astype(BF16).astype(F32)
        lo = v_all - hi - mid
        wmats = [jnp.concatenate([per_head_rows(z[p:p + 1], n) for z in (hi, mid, lo)], axis=1) for p in range(npair)]
        vcols = [_dot_nt(eye3, wmat) for wmat in wmats]
        for p in range(npair):
            vk_scr[p, t] = vcols[p] * k_all[p:p + 1]
        return carry

    lax.fori_loop(0, t_len, outer, 0)

    def step(t, carry):
        kk_all, a_all, w_all, r_all = kk_ref[0, t], a_ref[0, t], w_ref[0, t], r_ref[0, t]
        ka_all = kk_all * a_all
        pairs = range(npair)
        m_kk = [per_head_rows(-kk_all[p:p + 1], n) for p in pairs]
        m_r = [per_head_rows(r_all[p:p + 1], 8) for p in pairs]
        sa = [_dot_nt(sb_scr[p], m_kk[p]) for p in pairs]
        s = [s_scr[p] * w_all[p:p + 1] + sa[p] * ka_all[p:p + 1] + vk_scr[p, t] for p in pairs]
        sb = [z.astype(BF16) for z in s]
        for p in pairs:
            s_scr[p] = s[p]
            sb_scr[p] = sb[p]
        yy = [_dot_nt(m_r[p], jnp.concatenate([sb[p], sb[p]], axis=0)) for p in pairs]
        for p in pairs:
            y_ref[0, t, p:p + 1, :] = jnp.where(low, yy[p][0:1], yy[p][8:9])
        return carry

    lax.fori_loop(0, t_len, step, 0)

    @pl.when(c == pl.num_programs(1) - 1)
    def _():
        st_ref[0] = s_scr[...]


def rwkv_seq(r, w, k, v, kk, a, s0, t_blk=RWKV_SEQ_T):
    b, l, d = r.shape
    npair = d // LANES
    t_blk = min(t_blk, l)
    assert l % t_blk == 0
    seq = pl.BlockSpec((1, t_blk, npair, LANES), lambda bi, ci: (bi, ci, 0, 0))
    sspec = pl.BlockSpec((1, npair, B_HEAD, LANES), lambda bi, ci: (bi, 0, 0, 0))
    y, s_new = pl.pallas_call(
        _rwkv_seq_kernel, name="rwkv_seq", grid=(b, l // t_blk),
        in_specs=[seq] * 6 + [sspec],
        out_specs=[seq, sspec],
        out_shape=[jax.ShapeDtypeStruct((b, l, npair, LANES), F32), jax.ShapeDtypeStruct(s0.shape, F32)],
        scratch_shapes=[pltpu.VMEM((npair, B_HEAD, LANES), F32), pltpu.VMEM((npair, B_HEAD, LANES), BF16),
                        pltpu.VMEM((npair, t_blk, B_HEAD, LANES), F32)],
        compiler_params=_cparams(("parallel", "arbitrary")))(
            *[z.reshape(b, l, npair, LANES) for z in (r, w, k, v, kk, a)], s0)
    return y.reshape(b, l, d), s_new


def _pair_states(s):
    b = s.shape[0]
    s = s.reshape(b, B_HEADS // 2, 2, B_HEAD, B_HEAD)
    return jnp.swapaxes(s, 2, 3).reshape(b, B_HEADS // 2, B_HEAD, 2 * B_HEAD)


def _unpair_states(sp):
    b = sp.shape[0]
    s = sp.reshape(b, B_HEADS // 2, B_HEAD, 2, B_HEAD)
    return jnp.swapaxes(s, 2, 3).reshape(b, B_HEADS, B_HEAD, B_HEAD)


def _pack_states(s):
    b = s.shape[0]
    s = s.reshape(b, B_HEADS // 2, 2, B_HEAD, B_HEAD)
    z = jnp.zeros_like(s[:, :, 0])
    top = jnp.concatenate([s[:, :, 0], z], axis=-1)
    bot = jnp.concatenate([z, s[:, :, 1]], axis=-1)
    return jnp.concatenate([top, bot], axis=-2)


def _unpack_states(sp):
    b = sp.shape[0]
    s0 = sp[:, :, :B_HEAD, :B_HEAD]
    s1 = sp[:, :, B_HEAD:, B_HEAD:]
    return jnp.stack([s0, s1], axis=2).reshape(b, B_HEADS, B_HEAD, B_HEAD)


def rwkv7_mix(pb, shift_prev, wkv_prev, lp):
    b, t, _ = pb.shape
    prev = jnp.concatenate([shift_prev[:, None], pb[:, :-1]], axis=1)
    xm = pb + (prev - pb) * lp['mu']
    splits = [B_DIM, 2 * B_DIM, 3 * B_DIM, 3 * B_DIM + B_DECAY_RANK, 3 * B_DIM + B_DECAY_RANK + B_ICLR_RANK]
    r, k, v, wl, al, gl = jnp.split(xm, splits, axis=-1)
    logw = -jax.nn.softplus(-(lp['w0'] + jnp.tanh(wl) @ lp['w2'])) - 0.5
    lw = -jnp.exp(logw)
    a = jax.nn.sigmoid(lp['a0'] + al @ lp['a2'])
    g = jax.nn.sigmoid(gl) @ lp['g2']
    heads = lambda z: z.reshape(b, t, B_HEADS, B_HEAD)
    kkh = heads(k * lp['k_k'])
    kkh = kkh * lax.rsqrt(jnp.sum(kkh * kkh, axis=-1, keepdims=True) + 1e-12)
    kk = kkh.reshape(b, t, B_DIM)
    k2 = k * (1.0 + (a - 1.0) * lp['k_a'])
    t_head = max(t - RWKV_TAIL, 0) // RWKV_T * RWKV_T
    seqs = [r, lw, k2, v, kk, a]
    state = wkv_prev
    ys = []
    if t_head:
        y_head, s_bd = rwkv_scan(*[z[:, :t_head] for z in seqs], _pack_states(state))
        state = _unpack_states(s_bd)
        ys.append(y_head)
    seqs[1] = jnp.exp(lw)
    y_tail, s_pair = rwkv_seq(*[z[:, t_head:] for z in seqs], _pair_states(state))
    ys.append(y_tail)
    y = heads(jnp.concatenate(ys, axis=1))
    mu = jnp.mean(y, axis=-1, keepdims=True)
    var = jnp.mean(jnp.square(y - mu), axis=-1, keepdims=True)
    yn = ((y - mu) * lax.rsqrt(var + B_GN_EPS)).reshape(b, t, B_DIM) * lp['ln_g'] + lp['ln_b']
    bonus = jnp.sum(heads(r) * heads(k2) * lp['r_k'], axis=-1, keepdims=True) * heads(v)
    o = (yn + bonus.reshape(b, t, B_DIM)) * g
    return o, pb[:, -1], _unpair_states(s_pair)


def _conv_stage(u, st_ref, ns_ref, xbuf, *, halo, bf16_taps):
    i = pl.program_id(1)
    rows = u.shape[0]

    @pl.when(i == 0)
    def _():
        xbuf[0:halo, :] = st_ref[0]

    @pl.when(i > 0)
    def _():
        xbuf[0:halo, :] = xbuf[rows:rows + halo, :]

    xbuf[halo:halo + rows, :] = u.astype(BF16).astype(F32) if bf16_taps else u

    @pl.when(i == pl.num_programs(1) - 1)
    def _():
        ns_ref[0] = u[rows - halo:, :]


def _conv_taps(xbuf, w_ref, rows, cols, *, width, halo):
    acc = None
    for j in range(width):
        term = xbuf[pl.ds(halo - (width - 1) + j, rows), cols] * w_ref[j:j + 1, cols]
        acc = term if acc is None else acc + term
    return acc


def _sconv_kernel(bg_ref, cg_ref, hc_ref, st_ref, w_ref, o_ref, ns_ref, xbuf, *, bf16_taps):
    rows = o_ref.shape[0]
    _conv_stage(cg_ref[...] * hc_ref[...], st_ref, ns_ref, xbuf, halo=CONV_C_HALO, bf16_taps=bf16_taps)
    for c0 in range(0, o_ref.shape[1], CONV_STRIP):
        cols = slice(c0, c0 + CONV_STRIP)
        y = _conv_taps(xbuf, w_ref, rows, cols, width=C_WIDTH, halo=CONV_C_HALO)
        o_ref[:, cols] = (bg_ref[:, cols] * y).astype(o_ref.dtype)


def _cconv_kernel(ga_ref, gb_ref, st_ref, w_ref, b_ref, g_ref, beta_ref, o_ref, ns_ref, xbuf, ybuf, *, bf16_taps):
    rows = o_ref.shape[0]
    _conv_stage(ga_ref[...] * jax.nn.sigmoid(gb_ref[...]), st_ref, ns_ref, xbuf,
                halo=CONV_D_HALO, bf16_taps=bf16_taps)
    for c0 in range(0, o_ref.shape[1], CONV_STRIP):
        cols = slice(c0, c0 + CONV_STRIP)
        ybuf[:, cols] = _conv_taps(xbuf, w_ref, rows, cols, width=D_WIDTH, halo=CONV_D_HALO) + b_ref[:, cols]
    z = ybuf[...]
    mu = jnp.mean(z, axis=-1, keepdims=True)
    var = jnp.mean(jnp.square(z - mu), axis=-1, keepdims=True)
    zn = (z - mu) * lax.rsqrt(var + NORM_EPS) * g_ref[...] + beta_ref[...]
    o_ref[...] = (zn * jax.nn.sigmoid(zn)).astype(o_ref.dtype)


def _conv_call(kernel_fn, name, srcs, state, consts, *, width, halo, first_row, rows, bf16_taps, extra_scratch):
    nseq = state.shape[0]
    d = D_MODEL
    hist = width - 1
    seq_len = N_PROMPT if nseq == 1 else DEC_SEQ
    assert seq_len % rows == 0 and rows >= halo >= hist
    tiles = seq_len // rows
    base = first_row // rows
    row_blk = lambda b, i: base + b * tiles + i
    in_specs = [pl.BlockSpec((rows, d), functools.partial(lambda b, i, c: (row_blk(b, i), c), c=col))
                for _, col in srcs]
    in_specs.append(pl.BlockSpec((1, halo, d), lambda b, i: (b, 0, 0)))
    in_specs += [pl.BlockSpec(c.shape, lambda b, i: (0, 0)) for c in consts]
    state_pad = jnp.pad(state, ((0, 0), (halo - hist, 0), (0, 0)))
    out, tail = pl.pallas_call(
        functools.partial(kernel_fn, bf16_taps=bf16_taps), name=name, grid=(nseq, tiles),
        in_specs=in_specs,
        out_specs=[pl.BlockSpec((rows, d), lambda b, i: (b * tiles + i, 0)),
                   pl.BlockSpec((1, halo, d), lambda b, i: (b, 0, 0))],
        out_shape=[jax.ShapeDtypeStruct((nseq * seq_len, d), BF16), jax.ShapeDtypeStruct((nseq, halo, d), F32)],
        scratch_shapes=[pltpu.VMEM((halo + rows, d), F32)] + extra_scratch(rows),
        compiler_params=_cparams(("parallel", "arbitrary")))(*[a for a, _ in srcs], state_pad, *consts)
    return out, tail[:, halo - hist:]


def short_conv(bch, state, w, *, first_row, rows, bf16_taps):
    return _conv_call(_sconv_kernel, "sconv", [(bch, 0), (bch, 1), (bch, 2)], state, [w],
                      width=C_WIDTH, halo=CONV_C_HALO, first_row=first_row, rows=rows, bf16_taps=bf16_taps,
                      extra_scratch=lambda r: [])


def conformer_conv(pd, state, w, b, g, beta, *, first_row, rows, bf16_taps):
    row = lambda z: z.reshape(1, D_MODEL)
    return _conv_call(_cconv_kernel, "cconv", [(pd, 0), (pd, 1)], state, [w, row(b), row(g), row(beta)],
                      width=D_WIDTH, halo=CONV_D_HALO, first_row=first_row, rows=rows, bf16_taps=bf16_taps,
                      extra_scratch=lambda r: [pltpu.VMEM((r, D_MODEL), F32)])


def _moe_up_kernel(be_ref, nb_ref, x_ref, w_ref, b_ref, o_ref):
    i = pl.program_id(0)

    @pl.when(i < nb_ref[0])
    def _():
        ff = o_ref.shape[1]
        tf = min(ff, MOE_FF_TILE)
        x = x_ref[...]
        for f in range(ff // tf):
            gcol = slice(f * tf, (f + 1) * tf)
            ucol = slice(ff + f * tf, ff + (f + 1) * tf)
            gate = _dot(x, w_ref[0, 0, :, gcol]) + b_ref[0, 0, :, gcol]
            up = _dot(x, w_ref[0, 0, :, ucol]) + b_ref[0, 0, :, ucol]
            gate = jnp.minimum(gate, SWIGLU_LIMIT)
            up = jnp.clip(up, -SWIGLU_LIMIT, SWIGLU_LIMIT)
            act = gate * jax.nn.sigmoid(SWIGLU_ALPHA * gate) * (up + 1.0)
            o_ref[:, gcol] = act.astype(o_ref.dtype)

    @pl.when(i >= nb_ref[0])
    def _():
        o_ref[...] = jnp.zeros(o_ref.shape, o_ref.dtype)


def _moe_down_kernel(be_ref, nb_ref, a_ref, w_ref, b_ref, p_ref, o_ref):
    i = pl.program_id(0)

    @pl.when(i < nb_ref[0])
    def _():
        o_ref[...] = (_dot(a_ref[...], w_ref[0, 0]) + b_ref[0, 0]) * p_ref[...]

    @pl.when(i >= nb_ref[0])
    def _():
        o_ref[...] = jnp.zeros(o_ref.shape, o_ref.dtype)


def moe_experts(xb, row_p, block_e, n_used, layer, w_gu, b_gu, w_down, b_down):
    rows, d = xb.shape
    nblk = rows // MOE_BLK
    ff = w_down.shape[2]
    cp = pltpu.CompilerParams(dimension_semantics=("arbitrary",), vmem_limit_bytes=MOE_VMEM_LIMIT)
    act = pl.pallas_call(
        _moe_up_kernel, name="moe_up",
        grid_spec=pltpu.PrefetchScalarGridSpec(
            num_scalar_prefetch=2, grid=(nblk,),
            in_specs=[pl.BlockSpec((MOE_BLK, d), lambda i, be, nb: (i, 0)),
                      pl.BlockSpec((1, 1, d, 2 * ff), lambda i, be, nb: (layer, be[i], 0, 0)),
                      pl.BlockSpec((1, 1, 1, 2 * ff), lambda i, be, nb: (layer, be[i], 0, 0))],
            out_specs=pl.BlockSpec((MOE_BLK, ff), lambda i, be, nb: (i, 0))),
        out_shape=jax.ShapeDtypeStruct((rows, ff), BF16),
        compiler_params=cp)(block_e, n_used, xb, w_gu, b_gu)
    return pl.pallas_call(
        _moe_down_kernel, name="moe_down",
        grid_spec=pltpu.PrefetchScalarGridSpec(
            num_scalar_prefetch=2, grid=(nblk,),
            in_specs=[pl.BlockSpec((MOE_BLK, ff), lambda i, be, nb: (i, 0)),
                      pl.BlockSpec((1, 1, ff, d), lambda i, be, nb: (layer, be[i], 0, 0)),
                      pl.BlockSpec((1, 1, 1, d), lambda i, be, nb: (layer, be[i], 0, 0)),
                      pl.BlockSpec((MOE_BLK, 1), lambda i, be, nb: (i, 0))],
            out_specs=pl.BlockSpec((MOE_BLK, d), lambda i, be, nb: (i, 0))),
        out_shape=jax.ShapeDtypeStruct((rows, d), F32),
        compiler_params=cp)(block_e, n_used, act, w_down, b_down, row_p)


def moe_ffn(h, logits, layer, w_gu, b_gu, w_down, b_down):
    n, d = h.shape
    top_val, top_idx = lax.top_k(logits, TOP_K)
    probs = jax.nn.softmax(top_val, axis=-1)
    flat_e = top_idx.reshape(-1).astype(jnp.int32)
    order = jnp.argsort(flat_e, stable=True).astype(jnp.int32)
    e_sorted = flat_e[order]
    counts = jnp.bincount(flat_e, length=N_EXPERTS).astype(jnp.int32)
    padded = (counts + MOE_BLK - 1) // MOE_BLK * MOE_BLK
    pad_end = jnp.cumsum(padded)
    pad_start = pad_end - padded
    start = jnp.cumsum(counts) - counts
    dest = pad_start[e_sorted] + jnp.arange(n * TOP_K, dtype=jnp.int32) - start[e_sorted]
    nblk = -(-(n * TOP_K + N_EXPERTS * (MOE_BLK - 1)) // MOE_BLK)
    rows = nblk * MOE_BLK
    row_tok = jnp.full((rows,), n, jnp.int32).at[dest].set(order // TOP_K)
    row_p = jnp.zeros((rows,), F32).at[dest].set(probs.reshape(-1)[order])
    block_e = jnp.minimum(jnp.searchsorted(pad_end, jnp.arange(nblk, dtype=jnp.int32) * MOE_BLK, side='right'),
                          N_EXPERTS - 1).astype(jnp.int32)
    n_used = (pad_end[-1] // MOE_BLK).astype(jnp.int32).reshape(1)
    xb = jnp.concatenate([h, jnp.zeros((1, d), h.dtype)], axis=0)[row_tok]
    yb = moe_experts(xb, row_p.reshape(rows, 1), block_e, n_used, layer, w_gu, b_gu, w_down, b_down)
    pos = jnp.zeros((n * TOP_K,), jnp.int32).at[order].set(dest)
    return jnp.sum(yb[pos].reshape(n, TOP_K, d), axis=1)


def _split_groups(z):
    return z[:N_PROMPT].reshape(1, SEQ, -1), z[N_PROMPT:].reshape(DEC_BATCH, DEC_SEQ, -1)


def _join_groups(zp, zs):
    return jnp.concatenate([zp.reshape(N_PROMPT, -1), zs.reshape(N_SAMPLE, -1)], axis=0)


def layer(x, mod_g, lp, st, attn_tabs):
    d = D_MODEL
    h = norm_mod(x, lp['norm1_g'], mod_g, 0, 1)
    qkv = matmul(h, lp['w_qkv'], name="proj_qkv", tm=768, tn=512)
    pb = matmul(h, lp['w_pb'], name="proj_rwkv", tm=768, tn=1280)
    bch = matmul(h, lp['w_bch'], name="proj_sconv", tm=768, tn=512)
    pd = matmul(h, lp['w_pd'], name="proj_cconv", tm=768, tn=512)
    gates = matmul(h, lp['w_gl'], name="proj_gates", tm=768, tn=512, act="sigmoid")

    (bias_p, sink_p), (bias_s, sink_s) = attn_tabs
    o_a = jnp.concatenate([
        attn_prompt(qkv, bias_p, sink_p, N_PROMPT),
        attn_sample(qkv, st['k'].reshape(DEC_BATCH, WINDOW, A_KV), st['v'].reshape(DEC_BATCH, WINDOW, A_KV),
                    bias_s, sink_s, N_PROMPT)], axis=0)
    kp, ks = _split_groups(qkv[:, OFF_K:OFF_V])
    vp, vs = _split_groups(qkv[:, OFF_V:OFF_PB])
    new_kv_p = (kp[:, SEQ - WINDOW:].reshape(1, WINDOW, A_KV_HEADS, A_HEAD_DIM),
                vp[:, SEQ - WINDOW:].reshape(1, WINDOW, A_KV_HEADS, A_HEAD_DIM))
    new_kv_s = (ks.reshape(DEC_BATCH, DEC_SEQ, A_KV_HEADS, A_HEAD_DIM),
                vs.reshape(DEC_BATCH, DEC_SEQ, A_KV_HEADS, A_HEAD_DIM))

    pb_p, pb_s = _split_groups(pb)
    ob_p, shift_p, wkv_p = rwkv7_mix(pb_p, jnp.zeros((1, B_PROJ), F32),
                                     jnp.zeros((1, B_HEADS, B_HEAD, B_HEAD), F32), lp)
    ob_s, shift_s, wkv_s = rwkv7_mix(pb_s, st['shift'], st['wkv'], lp)
    o_b = _join_groups(ob_p, ob_s)

    ocp, sconv_p = short_conv(bch, jnp.zeros((1, C_WIDTH - 1, d), F32), lp['sconv_w'],
                              first_row=0, rows=CONV_ROWS, bf16_taps=True)
    ocs, sconv_s = short_conv(bch, st['sconv'], lp['sconv_w'], first_row=N_PROMPT, rows=DEC_SEQ, bf16_taps=False)
    o_c = jnp.concatenate([ocp, ocs], axis=0)
    cargs = (lp['cconv_w'], lp['cconv_b'], lp['cnorm_g'], lp['cnorm_b'])
    odp, cconv_p = conformer_conv(pd, jnp.zeros((1, D_WIDTH - 1, d), F32), *cargs,
                                  first_row=0, rows=CONV_ROWS, bf16_taps=True)
    ods, cconv_s = conformer_conv(pd, st['cconv'], *cargs, first_row=N_PROMPT, rows=DEC_SEQ, bf16_taps=False)
    o_d = jnp.concatenate([odp, ods], axis=0)

    merged = merge_branches([o_a, o_b.astype(BF16), o_c, o_d], lp['w_branch'], lp['layer'], gates)
    x = matmul_resid(merged, lp['w_out'], x, mod_g, 2)

    h2, logits = norm_mod(x, lp['norm2_g'], mod_g, 3, 4, router=(lp['router_w'], lp['router_b']))
    ffn = moe_ffn(h2, logits, lp['layer'], lp['w_gu'], lp['b_gu'], lp['w_down'], lp['b_down'])
    x = resid_gate(x, ffn, mod_g, 5)
    new_p = (new_kv_p[0], new_kv_p[1], wkv_p, shift_p, sconv_p, cconv_p)
    new_s = (new_kv_s[0], new_kv_s[1], wkv_s, shift_s, sconv_s, cconv_s)
    return x, new_p, new_s


def kernel(x_prompt, x_sample, c_prompt, c_sample, cache_k, cache_v, state_wkv, state_shift, state_sconv, state_cconv, w_mod, b_mod, norm1_g, norm2_g, w_in, attn_sinks, rel_bias_table, rwkv_mu, rwkv_w0, rwkv_w2, rwkv_a0, rwkv_a2, rwkv_g2, rwkv_k_k, rwkv_k_a, rwkv_r_k, rwkv_ln_g, rwkv_ln_b, sconv_w, cconv_w, cconv_b, cnorm_g, cnorm_b, w_branch, w_out, router_w, router_b, expert_w_gu, expert_b_gu, expert_w_down, expert_b_down, final_g):
    d = D_MODEL
    x = jnp.concatenate([x_prompt.reshape(N_PROMPT, d), x_sample.reshape(N_SAMPLE, d)], axis=0)
    c_all = jnp.concatenate([c_prompt, c_sample], axis=0)
    c_act = jnp.pad(jax.nn.silu(c_all), ((0, 16 - c_all.shape[0]), (0, 0))).astype(BF16)

    blk_pos = jnp.arange(CHUNK, dtype=jnp.int32)
    qpos_p = WINDOW + blk_pos
    kpos_p = jnp.arange(WINDOW + CHUNK, dtype=jnp.int32)
    qpos_s = PAST_LEN + jnp.arange(DEC_SEQ, dtype=jnp.int32)
    kpos_s = PAST_LEN - WINDOW + jnp.arange(WINDOW + DEC_SEQ, dtype=jnp.int32)

    w_branch_bf = w_branch.astype(BF16)
    w_gu_bf = expert_w_gu.astype(BF16)
    w_down_bf = expert_w_down.astype(BF16)
    b_gu4 = expert_b_gu.reshape(DEPTH, N_EXPERTS, 1, -1)
    b_down4 = expert_b_down.reshape(DEPTH, N_EXPERTS, 1, -1)

    st_p, st_s = [], []
    for l in range(DEPTH):
        w_in_l = w_in[l]
        lp = dict(norm1_g=norm1_g[l], norm2_g=norm2_g[l],
                  w_qkv=w_in_l[:, OFF_Q:OFF_PB].astype(BF16), w_pb=w_in_l[:, OFF_PB:OFF_BG].astype(BF16),
                  w_bch=w_in_l[:, OFF_BG:OFF_PD].astype(BF16), w_pd=w_in_l[:, OFF_PD:OFF_GL].astype(BF16),
                  w_gl=w_in_l[:, OFF_GL:].astype(BF16),
                  mu=rwkv_mu[l], w0=rwkv_w0[l], w2=rwkv_w2[l], a0=rwkv_a0[l], a2=rwkv_a2[l],
                  g2=rwkv_g2[l], k_k=rwkv_k_k[l], k_a=rwkv_k_a[l], r_k=rwkv_r_k[l], ln_g=rwkv_ln_g[l],
                  ln_b=rwkv_ln_b[l], sconv_w=sconv_w[l], cconv_w=cconv_w[l], cconv_b=cconv_b[l],
                  cnorm_g=cnorm_g[l], cnorm_b=cnorm_b[l], layer=l, w_branch=w_branch_bf,
                  w_out=w_out[l].astype(BF16), router_w=router_w[l], router_b=router_b[l],
                  w_gu=w_gu_bf, b_gu=b_gu4, w_down=w_down_bf, b_down=b_down4)
        mod = matmul(c_act, w_mod[l].astype(BF16), name="modulation", tm=16, tn=512)[:c_all.shape[0]] + b_mod[l]
        mod_g = jnp.concatenate([jnp.broadcast_to(mod[:1], (N_PROMPT // MOD_GROUP, 6 * d)),
                                 jnp.repeat(mod[1:], DEC_SEQ // MOD_GROUP, axis=0)], axis=0)
        tabs = (_attn_tables(qpos_p, kpos_p, rel_bias_table, attn_sinks[l], False),
                _attn_tables(qpos_s, kpos_s, rel_bias_table, attn_sinks[l], True))
        st = dict(k=cache_k[l], v=cache_v[l], shift=state_shift[l], wkv=state_wkv[l],
                  sconv=state_sconv[l], cconv=state_cconv[l])
        x, sp, ss = layer(x, mod_g, lp, st, tabs)
        st_p.append(sp)
        st_s.append(ss)
    y = final_norm(x, final_g)
    y_prompt = y[:N_PROMPT].reshape(1, SEQ, d)
    y_sample = y[N_PROMPT:].reshape(DEC_BATCH, DEC_SEQ, d)
    return (y_prompt, y_sample,
            jnp.stack([s[0] for s in st_p]), jnp.stack([s[1] for s in st_p]), jnp.stack([s[2] for s in st_p]),
            jnp.stack([s[3] for s in st_p]), jnp.stack([s[4] for s in st_p]), jnp.stack([s[5] for s in st_p]),
            jnp.stack([s[0] for s in st_s]), jnp.stack([s[1] for s in st_s]), jnp.stack([s[2] for s in st_s]),
            jnp.stack([s[3] for s in st_s]), jnp.stack([s[4] for s in st_s]), jnp.stack([s[5] for s in st_s]))
```

```python
import functools
import math

import jax
import jax.numpy as jnp
from jax import lax
from jax.experimental import pallas as pl
from jax.experimental.pallas import tpu as pltpu

F32 = jnp.float32
BF16 = jnp.bfloat16

D_MODEL = 2048
BATCH = 1
SEQ = 8192
DEPTH = 2
DEC_BATCH = 8
DEC_SEQ = 32
PAST_LEN = 1024
CHUNK = 64
NORM_EPS = 1e-5
NEG_INF = -1e30

A_HEAD_DIM = 64
A_HEADS = 32
A_KV_HEADS = 8
A_GROUP = 4
WINDOW = 128
WIN_CHUNKS = 2
NUM_BUCKETS = 32
MAX_DISTANCE = 128
A_Q = A_HEADS * A_HEAD_DIM
A_KV = A_KV_HEADS * A_HEAD_DIM

B_HEAD = 64
B_HEADS = 32
B_DIM = 2048
B_DECAY_RANK = 64
B_ICLR_RANK = 64
B_GATE_RANK = 128
B_GN_EPS = 64e-5
B_PROJ = 3 * B_DIM + B_DECAY_RANK + B_ICLR_RANK + B_GATE_RANK

C_WIDTH = 3
D_WIDTH = 31
N_BRANCH = 4

N_EXPERTS = 32
TOP_K = 4
D_FF = D_MODEL
SWIGLU_LIMIT = 7.0
SWIGLU_ALPHA = 1.702

OFF_Q = 0
OFF_K = OFF_Q + A_Q
OFF_V = OFF_K + A_KV
OFF_PB = OFF_V + A_KV
OFF_BG = OFF_PB + B_PROJ
OFF_CG = OFF_BG + D_MODEL
OFF_HC = OFF_CG + D_MODEL
OFF_PD = OFF_HC + D_MODEL
OFF_GL = OFF_PD + 2 * D_MODEL
IN_TOTAL = OFF_GL + N_BRANCH * D_MODEL

N_PROMPT = SEQ
N_SAMPLE = DEC_BATCH * DEC_SEQ
N_TOK = N_PROMPT + N_SAMPLE
MOD_GROUP = 32
N_MOD_GROUPS = N_TOK // MOD_GROUP

LANES = 128
RWKV_T = 64
RWKV_PAIRS = 16
RWKV_SEQ_T = 16
RWKV_TAIL = 256
CONV_ROWS = 256
CONV_STRIP = 512
CONV_C_HALO = 8
CONV_D_HALO = 32
MOE_BLK = 256
MOE_FF_TILE = 512
VMEM_LIMIT = 48 * 1024 * 1024
MOE_VMEM_LIMIT = 56 * 1024 * 1024


def _cparams(sem):
    return pltpu.CompilerParams(dimension_semantics=sem, vmem_limit_bytes=VMEM_LIMIT)


def _dot(a, b):
    return jnp.dot(a, b, preferred_element_type=F32)


def _dot_nt(a, b):
    return lax.dot_general(a, b, (((1,), (1,)), ((), ())), preferred_element_type=F32)


def _mm_kernel(x_ref, w_ref, o_ref, *, act):
    y = _dot(x_ref[...].astype(BF16), w_ref[...])
    if act == "sigmoid":
        y = jax.nn.sigmoid(y)
    o_ref[...] = y.astype(o_ref.dtype)


def matmul(x, w, *, name, tm, tn, out_dtype=F32, act=None):
    m, k = x.shape
    n = w.shape[1]
    assert m % tm == 0 and n % tn == 0
    return pl.pallas_call(
        functools.partial(_mm_kernel, act=act), name=name,
        grid=(m // tm, n // tn),
        in_specs=[pl.BlockSpec((tm, k), lambda i, j: (i, 0)),
                  pl.BlockSpec((k, tn), lambda i, j: (0, j))],
        out_specs=pl.BlockSpec((tm, tn), lambda i, j: (i, j)),
        out_shape=jax.ShapeDtypeStruct((m, n), out_dtype),
        compiler_params=_cparams(("parallel", "parallel")),
    )(x, w)


def _group_rows(gi):
    return slice(gi * MOD_GROUP, (gi + 1) * MOD_GROUP)


def _norm_mod_rows(x_ref, g_ref, sh_ref, sc_ref, gi):
    x = x_ref[_group_rows(gi), :]
    y = x * lax.rsqrt(jnp.mean(x * x, axis=-1, keepdims=True) + NORM_EPS) * g_ref[...]
    return y * (1.0 + sc_ref[gi:gi + 1, :]) + sh_ref[gi:gi + 1, :]


def _norm_mod_kernel(x_ref, g_ref, sh_ref, sc_ref, h_ref):
    for gi in range(x_ref.shape[0] // MOD_GROUP):
        h_ref[_group_rows(gi), :] = _norm_mod_rows(x_ref, g_ref, sh_ref, sc_ref, gi).astype(h_ref.dtype)


def _norm_mod_router_kernel(x_ref, g_ref, sh_ref, sc_ref, rw_ref, rb_ref, h_ref, lg_ref):
    w = rw_ref[...].astype(BF16)
    for gi in range(x_ref.shape[0] // MOD_GROUP):
        h = _norm_mod_rows(x_ref, g_ref, sh_ref, sc_ref, gi).astype(BF16)
        h_ref[_group_rows(gi), :] = h.astype(h_ref.dtype)
        lg_ref[_group_rows(gi), :] = _dot(h, w) + rb_ref[...]


def norm_mod(x, g, mod_g, shift_idx, scale_idx, router=None, tm=256):
    m, d = x.shape
    gm = tm // MOD_GROUP
    in_specs = [pl.BlockSpec((tm, d), lambda i: (i, 0)),
                pl.BlockSpec((1, d), lambda i: (0, 0)),
                pl.BlockSpec((gm, d), lambda i: (i, shift_idx)),
                pl.BlockSpec((gm, d), lambda i: (i, scale_idx))]
    args = [x, g.reshape(1, d), mod_g, mod_g]
    if router is None:
        return pl.pallas_call(
            _norm_mod_kernel, name="norm_mod", grid=(m // tm,), in_specs=in_specs,
            out_specs=pl.BlockSpec((tm, d), lambda i: (i, 0)),
            out_shape=jax.ShapeDtypeStruct((m, d), BF16),
            compiler_params=_cparams(("parallel",)))(*args)
    rw, rb = router
    ne = rw.shape[1]
    in_specs += [pl.BlockSpec((d, ne), lambda i: (0, 0)), pl.BlockSpec((1, ne), lambda i: (0, 0))]
    args += [rw, rb.reshape(1, ne)]
    return pl.pallas_call(
        _norm_mod_router_kernel, name="norm_router", grid=(m // tm,), in_specs=in_specs,
        out_specs=[pl.BlockSpec((tm, d), lambda i: (i, 0)), pl.BlockSpec((tm, ne), lambda i: (i, 0))],
        out_shape=[jax.ShapeDtypeStruct((m, d), F32), jax.ShapeDtypeStruct((m, ne), F32)],
        compiler_params=_cparams(("parallel",)))(*args)


def _final_norm_kernel(x_ref, g_ref, o_ref):
    x = x_ref[...]
    o_ref[...] = x * lax.rsqrt(jnp.mean(x * x, axis=-1, keepdims=True) + NORM_EPS) * g_ref[...]


def final_norm(x, g, tm=256):
    m, d = x.shape
    return pl.pallas_call(
        _final_norm_kernel, name="final_norm", grid=(m // tm,),
        in_specs=[pl.BlockSpec((tm, d), lambda i: (i, 0)), pl.BlockSpec((1, d), lambda i: (0, 0))],
        out_specs=pl.BlockSpec((tm, d), lambda i: (i, 0)),
        out_shape=jax.ShapeDtypeStruct((m, d), F32),
        compiler_params=_cparams(("parallel",)))(x, g.reshape(1, d))


def _mm_resid_kernel(a_ref, w_ref, x_ref, gate_ref, o_ref):
    y = _dot(a_ref[...], w_ref[...])
    for gi in range(y.shape[0] // MOD_GROUP):
        rows = _group_rows(gi)
        o_ref[rows, :] = x_ref[rows, :] + gate_ref[gi:gi + 1, :] * y[rows]


def matmul_resid(a, w, x, mod_g, gate_idx, *, tm=768, tn=512):
    m, k = a.shape
    n = w.shape[1]
    gm = tm // MOD_GROUP
    nj = n // tn
    return pl.pallas_call(
        _mm_resid_kernel, name="out_proj", grid=(m // tm, nj),
        in_specs=[pl.BlockSpec((tm, k), lambda i, j: (i, 0)),
                  pl.BlockSpec((k, tn), lambda i, j: (0, j)),
                  pl.BlockSpec((tm, tn), lambda i, j: (i, j)),
                  pl.BlockSpec((gm, tn), lambda i, j: (i, gate_idx * nj + j))],
        out_specs=pl.BlockSpec((tm, tn), lambda i, j: (i, j)),
        out_shape=jax.ShapeDtypeStruct((m, n), F32),
        compiler_params=_cparams(("parallel", "parallel")))(a, w, x, mod_g)


def _resid_kernel(x_ref, f_ref, gate_ref, o_ref):
    d = x_ref.shape[1]
    for gi in range(x_ref.shape[0] // MOD_GROUP):
        rows = _group_rows(gi)
        ffn = f_ref[rows, 0:d]
        for k in range(1, TOP_K):
            ffn = ffn + f_ref[rows, k * d:(k + 1) * d]
        o_ref[rows, :] = x_ref[rows, :] + gate_ref[gi:gi + 1, :] * ffn


def resid_gate(x, f_pairs, mod_g, gate_idx, tm=256):
    m, d = x.shape
    gm = tm // MOD_GROUP
    return pl.pallas_call(
        _resid_kernel, name="ffn_resid", grid=(m // tm,),
        in_specs=[pl.BlockSpec((tm, d), lambda i: (i, 0)), pl.BlockSpec((tm, TOP_K * d), lambda i: (i, 0)),
                  pl.BlockSpec((gm, d), lambda i: (i, gate_idx))],
        out_specs=pl.BlockSpec((tm, d), lambda i: (i, 0)),
        out_shape=jax.ShapeDtypeStruct((m, d), F32),
        compiler_params=_cparams(("parallel",)))(x, f_pairs.reshape(m, TOP_K * d), mod_g)


def _merge_kernel(ba_ref, bb_ref, bc_ref, bd_ref, w_ref, ga_ref, gb_ref, gc_ref, gd_ref, o_ref):
    acc = None
    for n, (b_ref, g_ref) in enumerate(((ba_ref, ga_ref), (bb_ref, gb_ref), (bc_ref, gc_ref), (bd_ref, gd_ref))):
        part = g_ref[...] * _dot(b_ref[...], w_ref[0, n])
        acc = part if acc is None else acc + part
    o_ref[...] = acc.astype(o_ref.dtype)


def merge_branches(branches, w_branch, layer, gates, *, tm=768, tn=256):
    m, k = branches[0].shape
    d = w_branch.shape[3]
    nj = d // tn
    bspec = pl.BlockSpec((tm, k), lambda i, j: (i, 0))
    gspec = lambda n: pl.BlockSpec((tm, tn), lambda i, j: (i, n * nj + j))
    return pl.pallas_call(
        _merge_kernel, name="merge", grid=(m // tm, nj),
        in_specs=[bspec] * N_BRANCH + [pl.BlockSpec((1, N_BRANCH, k, tn), lambda i, j: (layer, 0, 0, j))]
                 + [gspec(n) for n in range(N_BRANCH)],
        out_specs=pl.BlockSpec((tm, tn), lambda i, j: (i, j)),
        out_shape=jax.ShapeDtypeStruct((m, d), BF16),
        compiler_params=_cparams(("parallel", "parallel")))(*branches, w_branch, gates, gates, gates, gates)


def _attn_core(q, k, v, bias_ref, sink_ref, o_ref, key_ok):
    sq = q.shape[0]
    lane = lax.broadcasted_iota(jnp.int32, (1, LANES), 1)
    low = lane < A_HEAD_DIM
    q = q * (A_HEAD_DIM ** -0.5)
    groups = range(A_KV_HEADS)
    k2, v2 = [], []
    for kvp in range(A_KV_HEADS // 2):
        kp = k[:, kvp * LANES:(kvp + 1) * LANES]
        vp = v[:, kvp * LANES:(kvp + 1) * LANES]
        kp_sw = pltpu.roll(kp, A_HEAD_DIM, 1)
        vp_sw = pltpu.roll(vp, A_HEAD_DIM, 1)
        k2 += [jnp.where(low, kp, kp_sw).astype(BF16), jnp.where(low, kp_sw, kp).astype(BF16)]
        v2 += [jnp.where(low, vp, vp_sw).astype(BF16), jnp.where(low, vp_sw, vp).astype(BF16)]
    lhs = []
    for g in groups:
        parts = []
        for j in range(2):
            qp = q[:, (2 * g + j) * LANES:(2 * g + j + 1) * LANES]
            parts.append(jnp.where(low, qp, 0.0))
            parts.append(jnp.where(low, 0.0, qp))
        lhs.append(jnp.concatenate(parts, axis=0).astype(BF16))
    s = [_dot_nt(lhs[g], k2[g]) + bias_ref[g] for g in groups]
    if key_ok is not None:
        s = [jnp.where(key_ok, z, NEG_INF) for z in s]
    sink = [sink_ref[g] for g in groups]
    m = [jnp.maximum(jnp.max(s[g], axis=-1, keepdims=True), sink[g]) for g in groups]
    p = [jnp.exp(s[g] - m[g]) for g in groups]
    p = [p[g] / (jnp.sum(p[g], axis=-1, keepdims=True) + jnp.exp(sink[g] - m[g])) for g in groups]
    o2 = [_dot(p[g].astype(BF16), v2[g]) for g in groups]
    for g in groups:
        for j in range(2):
            oj = jnp.where(low, o2[g][(2 * j) * sq:(2 * j + 1) * sq], o2[g][(2 * j + 1) * sq:(2 * j + 2) * sq])
            o_ref[:, (2 * g + j) * LANES:(2 * g + j + 1) * LANES] = oj.astype(o_ref.dtype)


def _attn_prompt_kernel(q_ref, k0_ref, k1_ref, k2_ref, v0_ref, v1_ref, v2_ref, bias_ref, sink_ref, o_ref):
    c = pl.program_id(0)
    k = jnp.concatenate([k0_ref[...], k1_ref[...], k2_ref[...]], axis=0)
    v = jnp.concatenate([v0_ref[...], v1_ref[...], v2_ref[...]], axis=0)
    sk = k.shape[0]
    k_pos = (c - WIN_CHUNKS) * CHUNK + lax.broadcasted_iota(jnp.int32, (1, sk), 1)
    _attn_core(q_ref[...], k, v, bias_ref, sink_ref, o_ref, k_pos >= 0)


_QKV_KCOL = A_Q // A_KV
_QKV_VCOL = _QKV_KCOL + 1


def attn_prompt(qkv, bias, sink_col, n_prompt):
    nc = n_prompt // CHUNK
    kspec = lambda d, col: pl.BlockSpec((CHUNK, A_KV), lambda c: (jnp.maximum(c - d, 0), col))
    return pl.pallas_call(
        _attn_prompt_kernel, name="attn_prompt", grid=(nc,),
        in_specs=[pl.BlockSpec((CHUNK, A_Q), lambda c: (c, 0)),
                  kspec(2, _QKV_KCOL), kspec(1, _QKV_KCOL), kspec(0, _QKV_KCOL),
                  kspec(2, _QKV_VCOL), kspec(1, _QKV_VCOL), kspec(0, _QKV_VCOL),
                  pl.BlockSpec(bias.shape, lambda c: (0, 0, 0)),
                  pl.BlockSpec(sink_col.shape, lambda c: (0, 0, 0))],
        out_specs=pl.BlockSpec((CHUNK, A_Q), lambda c: (c, 0)),
        out_shape=jax.ShapeDtypeStruct((n_prompt, A_Q), BF16),
        compiler_params=_cparams(("parallel",)))(qkv, qkv, qkv, qkv, qkv, qkv, qkv, bias, sink_col)


def _attn_sample_kernel(q_ref, kn_ref, vn_ref, kc_ref, vc_ref, bias_ref, sink_ref, o_ref):
    k = jnp.concatenate([kc_ref[0], kn_ref[...]], axis=0)
    v = jnp.concatenate([vc_ref[0], vn_ref[...]], axis=0)
    _attn_core(q_ref[...], k, v, bias_ref, sink_ref, o_ref, None)


def attn_sample(qkv, k_cache, v_cache, bias, sink_col, n_prompt):
    nb = k_cache.shape[0]
    base = n_prompt // DEC_SEQ
    return pl.pallas_call(
        _attn_sample_kernel, name="attn_sample", grid=(nb,),
        in_specs=[pl.BlockSpec((DEC_SEQ, A_Q), lambda b: (base + b, 0)),
                  pl.BlockSpec((DEC_SEQ, A_KV), lambda b: (base + b, _QKV_KCOL)),
                  pl.BlockSpec((DEC_SEQ, A_KV), lambda b: (base + b, _QKV_VCOL)),
                  pl.BlockSpec((1, WINDOW, A_KV), lambda b: (b, 0, 0)),
                  pl.BlockSpec((1, WINDOW, A_KV), lambda b: (b, 0, 0)),
                  pl.BlockSpec(bias.shape, lambda b: (0, 0, 0)),
                  pl.BlockSpec(sink_col.shape, lambda b: (0, 0, 0))],
        out_specs=pl.BlockSpec((DEC_SEQ, A_Q), lambda b: (b, 0)),
        out_shape=jax.ShapeDtypeStruct((nb * DEC_SEQ, A_Q), BF16),
        compiler_params=_cparams(("parallel",)))(qkv, qkv, qkv, k_cache, v_cache, bias, sink_col)


def _t5_bucket(rel):
    nb = NUM_BUCKETS // 2
    max_exact = nb // 2
    ret = jnp.where(rel > 0, nb, 0)
    n = jnp.abs(rel)
    nf = jnp.maximum(n, 1).astype(F32)
    large = max_exact + (jnp.log(nf / max_exact) / math.log(MAX_DISTANCE / max_exact) * (nb - max_exact)).astype(jnp.int32)
    large = jnp.minimum(large, nb - 1)
    return ret + jnp.where(n < max_exact, n, large)


def _attn_tables(q_pos, k_pos, table, sinks, mask_positions):
    sq, sk = q_pos.shape[0], k_pos.shape[0]
    b = table[_t5_bucket(k_pos[None, :] - q_pos[:, None])]
    b = jnp.transpose(b, (2, 0, 1)).astype(F32)
    if mask_positions:
        qc, kc = q_pos[:, None] // CHUNK, k_pos[None, :] // CHUNK
        vis = (k_pos[None, :] >= 0) & (kc <= qc) & (kc >= qc - WIN_CHUNKS)
        b = jnp.where(vis[None], b, NEG_INF)
    bias = b.reshape(A_KV_HEADS, A_GROUP * sq, sk)
    sink_col = jnp.broadcast_to(sinks.astype(F32).reshape(A_KV_HEADS, A_GROUP, 1, 1),
                                (A_KV_HEADS, A_GROUP, sq, 1)).reshape(A_KV_HEADS, A_GROUP * sq, 1)
    return bias, sink_col


def _rwkv_kernel(r_ref, lw_ref, k_ref, v_ref, kk_ref, a_ref, s0_ref, y_ref, st_ref, s_scr):
    c = pl.program_id(2)
    t = RWKV_T

    @pl.when(c == 0)
    def _():
        s_scr[...] = s0_ref[0]

    row = lax.broadcasted_iota(jnp.int32, (t, LANES), 0)
    lane = lax.broadcasted_iota(jnp.int32, (t, LANES), 1)
    low = lane < B_HEAD
    col = lane & (t - 1)
    strict = col < row
    incl = col <= row
    tri = jnp.where(lax.broadcasted_iota(jnp.int32, (t, t), 1) <= lax.broadcasted_iota(jnp.int32, (t, t), 0),
                    1.0, 0.0).astype(BF16)
    r128 = lax.broadcasted_iota(jnp.int32, (LANES, LANES), 0)
    c128 = lax.broadcasted_iota(jnp.int32, (LANES, LANES), 1)
    same_head = (r128 < B_HEAD) == (c128 < B_HEAD)

    def stack(x):
        return jnp.concatenate([jnp.where(low, x, 0.0), jnp.where(low, 0.0, x)], axis=0)

    pairs = range(RWKV_PAIRS)
    sls = [slice(p * LANES, (p + 1) * LANES) for p in pairs]
    n_steps = int(math.log2(t))

    def prepare(sl):
        lw = lw_ref[0, :, sl]
        hi = lw.astype(BF16)
        rem = lw - hi.astype(F32)
        mid = rem.astype(BF16)
        lo = (rem - mid.astype(F32)).astype(BF16)
        cs = _dot(tri, hi) + _dot(tri, mid) + _dot(tri, lo)
        cs_end = cs[t - 1:t, :]
        kk = kk_ref[0, :, sl]
        ka = kk * a_ref[0, :, sl]
        kx = k_ref[0, :, sl]
        g_inv = jnp.exp(-cs)
        g_end = jnp.exp(cs_end - cs)
        a_t = -kk * jnp.exp(cs - lw)
        r_t = r_ref[0, :, sl] * jnp.exp(cs)
        lhs = jnp.concatenate([a_t, r_t], axis=0).astype(BF16)
        rhs = jnp.concatenate([stack(ka * g_inv), stack(kx * g_inv)], axis=0).astype(BF16)
        tail = jnp.concatenate([ka * g_end, kx * g_end], axis=0).astype(BF16)
        return lhs, rhs, tail, jnp.exp(cs_end)

    prep = [prepare(sl) for sl in sls]
    vs = [v_ref[0, :, sl] for sl in sls]
    st_v = [stack(v).astype(BF16) for v in vs]
    ss = [s_scr[p] for p in pairs]
    d1 = [_dot_nt(prep[p][0], prep[p][1]) for p in pairs]
    d2 = [_dot_nt(prep[p][0], ss[p].astype(BF16)) for p in pairs]
    l_ak = [jnp.where(strict, d1[p][:t, 2 * t:], 0.0).astype(BF16) for p in pairs]
    pw = [jnp.where(strict, d1[p][:t, :2 * t], 0.0) for p in pairs]
    l_r = [jnp.concatenate([jnp.where(incl, d1[p][t:, :2 * t], 0.0), jnp.where(incl, d1[p][t:, 2 * t:], 0.0)],
                           axis=1).astype(BF16) for p in pairs]
    u = [d2[p][:t] + _dot(l_ak[p], st_v[p]) for p in pairs]
    for i in range(n_steps):
        u = [u[p] + _dot(pw[p].astype(BF16), stack(u[p]).astype(BF16)) for p in pairs]
        if i < n_steps - 1:
            pw = [_dot(pw[p].astype(BF16), stack(pw[p]).astype(BF16)) for p in pairs]
    y = [d2[p][t:] + _dot(l_r[p], jnp.concatenate([stack(u[p]).astype(BF16), st_v[p]], axis=0)) for p in pairs]
    upd = [_dot(jnp.concatenate([u[p], vs[p]], axis=0).T.astype(BF16), prep[p][2]) for p in pairs]
    for p in pairs:
        s_scr[p] = ss[p] * prep[p][3] + jnp.where(same_head, upd[p], 0.0)
        y_ref[0, :, sls[p]] = y[p]

    @pl.when(c == pl.num_programs(2) - 1)
    def _():
        st_ref[0] = s_scr[...]


def rwkv_scan(r, lw, k, v, kk, a, s0):
    b, l, d = r.shape
    npair = d // LANES
    ng = npair // RWKV_PAIRS
    w = RWKV_PAIRS * LANES
    seq = pl.BlockSpec((1, RWKV_T, w), lambda bi, gi, ci: (bi, ci, gi))
    sspec = pl.BlockSpec((1, RWKV_PAIRS, LANES, LANES), lambda bi, gi, ci: (bi, gi, 0, 0))
    return pl.pallas_call(
        _rwkv_kernel, name="rwkv_chunk", grid=(b, ng, l // RWKV_T),
        in_specs=[seq] * 6 + [sspec],
        out_specs=[seq, sspec],
        out_shape=[jax.ShapeDtypeStruct((b, l, d), F32), jax.ShapeDtypeStruct(s0.shape, F32)],
        scratch_shapes=[pltpu.VMEM((RWKV_PAIRS, LANES, LANES), F32)],
        compiler_params=_cparams(("parallel", "parallel", "arbitrary")))(r, lw, k, v, kk, a, s0)


def _rwkv_seq_kernel(r_ref, w_ref, k_ref, v_ref, kk_ref, a_ref, s0_ref, y_ref, st_ref, s_scr, sb_scr, vk_scr):
    c = pl.program_id(1)
    t_len = r_ref.shape[1]
    npair = r_ref.shape[2]
    n = B_HEAD

    @pl.when(c == 0)
    def _():
        s_scr[...] = s0_ref[0]
        sb_scr[...] = s0_ref[0].astype(BF16)

    low = lax.broadcasted_iota(jnp.int32, (1, LANES), 1) < n
    eye = jnp.where((lax.broadcasted_iota(jnp.int32, (n, LANES), 1) & (n - 1))
                    == lax.broadcasted_iota(jnp.int32, (n, LANES), 0), 1.0, 0.0).astype(BF16)
    eye3 = jnp.concatenate([eye, eye, eye], axis=1)

    def per_head_rows(x, rows):
        x0 = jnp.where(low, x, 0.0).astype(BF16)
        x1 = jnp.where(low, 0.0, x).astype(BF16)
        return jnp.concatenate([jnp.broadcast_to(x0, (rows, LANES)), jnp.broadcast_to(x1, (rows, LANES))], axis=0)

    def outer(t, carry):
        v_all, k_all = v_ref[0, t], k_ref[0, t]
        hi = v_all.astype(BF16).astype(F32)
        mid = (v_all - hi).astype(BF16).astype(F32)
        lo = v_all - hi - mid
        wmats = [jnp.concatenate([per_head_rows(z[p:p + 1], n) for z in (hi, mid, lo)], axis=1) for p in range(npair)]
        vcols = [_dot_nt(eye3, wmat) for wmat in wmats]
        for p in range(npair):
            vk_scr[p, t] = vcols[p] * k_all[p:p + 1]
        return carry

    lax.fori_loop(0, t_len, outer, 0)

    def step(t, carry):
        kk_all, a_all, w_all, r_all = kk_ref[0, t], a_ref[0, t], w_ref[0, t], r_ref[0, t]
        ka_all = kk_all * a_all
        pairs = range(npair)
        m_kk = [per_head_rows(-kk_all[p:p + 1], n) for p in pairs]
        m_r = [per_head_rows(r_all[p:p + 1], 8) for p in pairs]
        sa = [_dot_nt(sb_scr[p], m_kk[p]) for p in pairs]
        s = [s_scr[p] * w_all[p:p + 1] + sa[p] * ka_all[p:p + 1] + vk_scr[p, t] for p in pairs]
        sb = [z.astype(BF16) for z in s]
        for p in pairs:
            s_scr[p] = s[p]
            sb_scr[p] = sb[p]
        yy = [_dot_nt(m_r[p], jnp.concatenate([sb[p], sb[p]], axis=0)) for p in pairs]
        for p in pairs:
            y_ref[0, t, p:p + 1, :] = jnp.where(low, yy[p][0:1], yy[p][8:9])
        return carry

    lax.fori_loop(0, t_len, step, 0)

    @pl.when(c == pl.num_programs(1) - 1)
    def _():
        st_ref[0] = s_scr[...]


def rwkv_seq(r, w, k, v, kk, a, s0, t_blk=RWKV_SEQ_T):
    b, l, d = r.shape
    npair = d // LANES
    t_blk = min(t_blk, l)
    assert l % t_blk == 0
    seq = pl.BlockSpec((1, t_blk, npair, LANES), lambda bi, ci: (bi, ci, 0, 0))
    sspec = pl.BlockSpec((1, npair, B_HEAD, LANES), lambda bi, ci: (bi, 0, 0, 0))
    y, s_new = pl.pallas_call(
        _rwkv_seq_kernel, name="rwkv_seq", grid=(b, l // t_blk),
        in_specs=[seq] * 6 + [sspec],
        out_specs=[seq, sspec],
        out_shape=[jax.ShapeDtypeStruct((b, l, npair, LANES), F32), jax.ShapeDtypeStruct(s0.shape, F32)],
        scratch_shapes=[pltpu.VMEM((npair, B_HEAD, LANES), F32), pltpu.VMEM((npair, B_HEAD, LANES), BF16),
                        pltpu.VMEM((npair, t_blk, B_HEAD, LANES), F32)],
        compiler_params=_cparams(("arbitrary", "arbitrary")))(
            *[z.reshape(b, l, npair, LANES) for z in (r, w, k, v, kk, a)], s0)
    return y.reshape(b, l, d), s_new


def _pair_states(s):
    b = s.shape[0]
    s = s.reshape(b, B_HEADS // 2, 2, B_HEAD, B_HEAD)
    return jnp.swapaxes(s, 2, 3).reshape(b, B_HEADS // 2, B_HEAD, 2 * B_HEAD)


def _unpair_states(sp):
    b = sp.shape[0]
    s = sp.reshape(b, B_HEADS // 2, B_HEAD, 2, B_HEAD)
    return jnp.swapaxes(s, 2, 3).reshape(b, B_HEADS, B_HEAD, B_HEAD)


def _pack_states(s):
    b = s.shape[0]
    s = s.reshape(b, B_HEADS // 2, 2, B_HEAD, B_HEAD)
    z = jnp.zeros_like(s[:, :, 0])
    top = jnp.concatenate([s[:, :, 0], z], axis=-1)
    bot = jnp.concatenate([z, s[:, :, 1]], axis=-1)
    return jnp.concatenate([top, bot], axis=-2)


def _unpack_states(sp):
    b = sp.shape[0]
    s0 = sp[:, :, :B_HEAD, :B_HEAD]
    s1 = sp[:, :, B_HEAD:, B_HEAD:]
    return jnp.stack([s0, s1], axis=2).reshape(b, B_HEADS, B_HEAD, B_HEAD)


def rwkv7_mix(pb, shift_prev, wkv_prev, lp):
    b, t, _ = pb.shape
    prev = jnp.concatenate([shift_prev[:, None], pb[:, :-1]], axis=1)
    xm = pb + (prev - pb) * lp['mu']
    splits = [B_DIM, 2 * B_DIM, 3 * B_DIM, 3 * B_DIM + B_DECAY_RANK, 3 * B_DIM + B_DECAY_RANK + B_ICLR_RANK]
    r, k, v, wl, al, gl = jnp.split(xm, splits, axis=-1)
    logw = -jax.nn.softplus(-(lp['w0'] + jnp.tanh(wl) @ lp['w2'])) - 0.5
    lw = -jnp.exp(logw)
    a = jax.nn.sigmoid(lp['a0'] + al @ lp['a2'])
    g = jax.nn.sigmoid(gl) @ lp['g2']
    heads = lambda z: z.reshape(b, t, B_HEADS, B_HEAD)
    kkh = heads(k * lp['k_k'])
    kkh = kkh * lax.rsqrt(jnp.sum(kkh * kkh, axis=-1, keepdims=True) + 1e-12)
    kk = kkh.reshape(b, t, B_DIM)
    k2 = k * (1.0 + (a - 1.0) * lp['k_a'])
    t_head = max(t - RWKV_TAIL, 0) // RWKV_T * RWKV_T
    seqs = [r, lw, k2, v, kk, a]
    state = wkv_prev
    ys = []
    if t_head:
        y_head, s_bd = rwkv_scan(*[z[:, :t_head] for z in seqs], _pack_states(state))
        state = _unpack_states(s_bd)
        ys.append(y_head)
    seqs[1] = jnp.exp(lw)
    y_tail, s_pair = rwkv_seq(*[z[:, t_head:] for z in seqs], _pair_states(state))
    ys.append(y_tail)
    y = heads(jnp.concatenate(ys, axis=1))
    mu = jnp.mean(y, axis=-1, keepdims=True)
    var = jnp.mean(jnp.square(y - mu), axis=-1, keepdims=True)
    yn = ((y - mu) * lax.rsqrt(var + B_GN_EPS)).reshape(b, t, B_DIM) * lp['ln_g'] + lp['ln_b']
    bonus = jnp.sum(heads(r) * heads(k2) * lp['r_k'], axis=-1, keepdims=True) * heads(v)
    o = (yn + bonus.reshape(b, t, B_DIM)) * g
    return o, pb[:, -1], _unpair_states(s_pair)


def _round_bf16(x):
    return x.astype(BF16).astype(F32)


def _conv_stage(u, st_ref, ns_ref, xbuf, *, halo):
    i = pl.program_id(1)
    rows = u.shape[0]

    @pl.when(i == 0)
    def _():
        xbuf[0:halo, :] = _round_bf16(st_ref[0])

    @pl.when(i > 0)
    def _():
        xbuf[0:halo, :] = xbuf[rows:rows + halo, :]

    xbuf[halo:halo + rows, :] = _round_bf16(u)

    @pl.when(i == pl.num_programs(1) - 1)
    def _():
        ns_ref[0] = u[rows - halo:, :]


def _conv_taps(xbuf, w_ref, rows, cols, *, width, halo):
    acc = None
    for j in range(width):
        term = xbuf[pl.ds(halo - (width - 1) + j, rows), cols] * _round_bf16(w_ref[j:j + 1, cols])
        acc = term if acc is None else acc + term
    return acc


def _sconv_kernel(bg_ref, cg_ref, hc_ref, st_ref, w_ref, o_ref, ns_ref, xbuf):
    rows = o_ref.shape[0]
    _conv_stage(cg_ref[...] * hc_ref[...], st_ref, ns_ref, xbuf, halo=CONV_C_HALO)
    for c0 in range(0, o_ref.shape[1], CONV_STRIP):
        cols = slice(c0, c0 + CONV_STRIP)
        y = _conv_taps(xbuf, w_ref, rows, cols, width=C_WIDTH, halo=CONV_C_HALO)
        o_ref[:, cols] = (bg_ref[:, cols] * y).astype(o_ref.dtype)


def _cconv_kernel(ga_ref, gb_ref, st_ref, w_ref, b_ref, g_ref, beta_ref, o_ref, ns_ref, xbuf, ybuf):
    rows = o_ref.shape[0]
    _conv_stage(ga_ref[...] * jax.nn.sigmoid(gb_ref[...]), st_ref, ns_ref, xbuf,
                halo=CONV_D_HALO)
    for c0 in range(0, o_ref.shape[1], CONV_STRIP):
        cols = slice(c0, c0 + CONV_STRIP)
        ybuf[:, cols] = _conv_taps(xbuf, w_ref, rows, cols, width=D_WIDTH, halo=CONV_D_HALO) + b_ref[:, cols]
    z = ybuf[...]
    mu = jnp.mean(z, axis=-1, keepdims=True)
    var = jnp.mean(jnp.square(z - mu), axis=-1, keepdims=True)
    zn = (z - mu) * lax.rsqrt(var + NORM_EPS) * g_ref[...] + beta_ref[...]
    o_ref[...] = (zn * jax.nn.sigmoid(zn)).astype(o_ref.dtype)


def _conv_call(kernel_fn, name, srcs, state, consts, *, width, halo, first_row, rows, extra_scratch):
    nseq = state.shape[0]
    d = D_MODEL
    hist = width - 1
    seq_len = N_PROMPT if nseq == 1 else DEC_SEQ
    assert seq_len % rows == 0 and rows >= halo >= hist
    tiles = seq_len // rows
    base = first_row // rows
    row_blk = lambda b, i: base + b * tiles + i
    in_specs = [pl.BlockSpec((rows, d), functools.partial(lambda b, i, c: (row_blk(b, i), c), c=col))
                for _, col in srcs]
    in_specs.append(pl.BlockSpec((1, halo, d), lambda b, i: (b, 0, 0)))
    in_specs += [pl.BlockSpec(c.shape, lambda b, i: (0, 0)) for c in consts]
    state_pad = jnp.pad(state, ((0, 0), (halo - hist, 0), (0, 0)))
    out, tail = pl.pallas_call(
        functools.partial(kernel_fn), name=name, grid=(nseq, tiles),
        in_specs=in_specs,
        out_specs=[pl.BlockSpec((rows, d), lambda b, i: (b * tiles + i, 0)),
                   pl.BlockSpec((1, halo, d), lambda b, i: (b, 0, 0))],
        out_shape=[jax.ShapeDtypeStruct((nseq * seq_len, d), BF16), jax.ShapeDtypeStruct((nseq, halo, d), F32)],
        scratch_shapes=[pltpu.VMEM((halo + rows, d), F32)] + extra_scratch(rows),
        compiler_params=_cparams(("parallel", "arbitrary")))(*[a for a, _ in srcs], state_pad, *consts)
    return out, tail[:, halo - hist:]


def short_conv(bch, state, w, *, first_row, rows):
    return _conv_call(_sconv_kernel, "sconv", [(bch, 0), (bch, 1), (bch, 2)], state, [w],
                      width=C_WIDTH, halo=CONV_C_HALO, first_row=first_row, rows=rows,
                      extra_scratch=lambda r: [])


def conformer_conv(pd, state, w, b, g, beta, *, first_row, rows):
    row = lambda z: z.reshape(1, D_MODEL)
    return _conv_call(_cconv_kernel, "cconv", [(pd, 0), (pd, 1)], state, [w, row(b), row(g), row(beta)],
                      width=D_WIDTH, halo=CONV_D_HALO, first_row=first_row, rows=rows,
                      extra_scratch=lambda r: [pltpu.VMEM((r, D_MODEL), F32)])


def _moe_up_kernel(be_ref, cnt_ref, x_ref, w_ref, b_ref, o_ref):
    i = pl.program_id(0)

    @pl.when(cnt_ref[i] > 0)
    def _():
        ff = o_ref.shape[1]
        tf = min(ff, MOE_FF_TILE)
        x = x_ref[...].astype(BF16)
        for f in range(ff // tf):
            gcol = slice(f * tf, (f + 1) * tf)
            ucol = slice(ff + f * tf, ff + (f + 1) * tf)
            gate = _dot(x, w_ref[0, 0, :, gcol]) + b_ref[0, 0, :, gcol]
            up = _dot(x, w_ref[0, 0, :, ucol]) + b_ref[0, 0, :, ucol]
            gate = jnp.minimum(gate, SWIGLU_LIMIT)
            up = jnp.clip(up, -SWIGLU_LIMIT, SWIGLU_LIMIT)
            act = gate * jax.nn.sigmoid(SWIGLU_ALPHA * gate) * (up + 1.0)
            o_ref[:, gcol] = act.astype(o_ref.dtype)

    @pl.when(cnt_ref[i] == 0)
    def _():
        o_ref[...] = jnp.zeros(o_ref.shape, o_ref.dtype)


def _row_copy(src_ref, src_row, dst_ref, dst_row, sem):
    return pltpu.make_async_copy(src_ref.at[pl.ds(src_row, 1)], dst_ref.at[pl.ds(dst_row, 1)], sem)


def _gather_rows_kernel(idx_ref, src_hbm, o_ref, sem):
    rows = o_ref.shape[0]

    def start(r, carry):
        _row_copy(src_hbm, idx_ref[0, 0, r], o_ref, r, sem).start()
        return carry

    def wait(r, carry):
        _row_copy(src_hbm, 0, o_ref, r, sem).wait()
        return carry

    lax.fori_loop(0, rows, start, 0)
    lax.fori_loop(0, rows, wait, 0)


def gather_rows(src, idx, blk):
    d = src.shape[1]
    nblk = idx.shape[0] // blk
    return pl.pallas_call(
        _gather_rows_kernel, name="moe_gather", grid=(nblk,),
        in_specs=[pl.BlockSpec((1, 1, blk), lambda i: (i, 0, 0), memory_space=pltpu.SMEM),
                  pl.BlockSpec(memory_space=pl.ANY)],
        out_specs=pl.BlockSpec((blk, d), lambda i: (i, 0)),
        out_shape=jax.ShapeDtypeStruct((nblk * blk, d), src.dtype),
        scratch_shapes=[pltpu.SemaphoreType.DMA(())],
        compiler_params=_cparams(("arbitrary",)))(idx.reshape(nblk, 1, blk), src)


def _moe_down_kernel(be_ref, cnt_ref, tgt_ref, a_ref, w_ref, b_ref, p_ref, y_hbm, ybuf, sem):
    i = pl.program_id(0)
    cnt = cnt_ref[i]

    @pl.when(cnt > 0)
    def _():
        ybuf[...] = (_dot(a_ref[...], w_ref[0, 0]) + b_ref[0, 0]) * p_ref[...]

        def start(r, carry):
            _row_copy(ybuf, r, y_hbm, tgt_ref[0, 0, r], sem).start()
            return carry

        def wait(r, carry):
            _row_copy(ybuf, r, y_hbm, 0, sem).wait()
            return carry

        lax.fori_loop(0, cnt, start, 0)
        lax.fori_loop(0, cnt, wait, 0)


def moe_experts(xb, row_p, row_tgt, block_e, block_cnt, n_pairs, layer, w_gu, b_gu, w_down, b_down):
    rows, d = xb.shape
    nblk = rows // MOE_BLK
    ff = w_down.shape[2]
    cp = pltpu.CompilerParams(dimension_semantics=("arbitrary",), vmem_limit_bytes=MOE_VMEM_LIMIT)
    act = pl.pallas_call(
        _moe_up_kernel, name="moe_up",
        grid_spec=pltpu.PrefetchScalarGridSpec(
            num_scalar_prefetch=2, grid=(nblk,),
            in_specs=[pl.BlockSpec((MOE_BLK, d), lambda i, be, cnt: (i, 0)),
                      pl.BlockSpec((1, 1, d, 2 * ff), lambda i, be, cnt: (layer, be[i], 0, 0)),
                      pl.BlockSpec((1, 1, 1, 2 * ff), lambda i, be, cnt: (layer, be[i], 0, 0))],
            out_specs=pl.BlockSpec((MOE_BLK, ff), lambda i, be, cnt: (i, 0))),
        out_shape=jax.ShapeDtypeStruct((rows, ff), BF16),
        compiler_params=cp)(block_e, block_cnt, xb, w_gu, b_gu)
    return pl.pallas_call(
        _moe_down_kernel, name="moe_down",
        grid_spec=pltpu.PrefetchScalarGridSpec(
            num_scalar_prefetch=2, grid=(nblk,),
            in_specs=[pl.BlockSpec((1, 1, MOE_BLK), lambda i, be, cnt: (i, 0, 0), memory_space=pltpu.SMEM),
                      pl.BlockSpec((MOE_BLK, ff), lambda i, be, cnt: (i, 0)),
                      pl.BlockSpec((1, 1, ff, d), lambda i, be, cnt: (layer, be[i], 0, 0)),
                      pl.BlockSpec((1, 1, 1, d), lambda i, be, cnt: (layer, be[i], 0, 0)),
                      pl.BlockSpec((MOE_BLK, 1), lambda i, be, cnt: (i, 0))],
            out_specs=pl.BlockSpec(memory_space=pl.ANY),
            scratch_shapes=[pltpu.VMEM((MOE_BLK, d), F32), pltpu.SemaphoreType.DMA(())]),
        out_shape=jax.ShapeDtypeStruct((n_pairs, d), F32),
        compiler_params=cp)(block_e, block_cnt, row_tgt.reshape(nblk, 1, MOE_BLK), act, w_down, b_down, row_p)


def moe_ffn(h, logits, layer, w_gu, b_gu, w_down, b_down):
    n, d = h.shape
    n_pairs = n * TOP_K
    top_val, top_idx = lax.top_k(logits, TOP_K)
    probs = jax.nn.softmax(top_val, axis=-1).reshape(-1)
    flat_e = top_idx.reshape(-1).astype(jnp.int32)
    order = jnp.argsort(flat_e, stable=True).astype(jnp.int32)
    counts = jnp.sum(flat_e[:, None] == jnp.arange(N_EXPERTS, dtype=jnp.int32)[None, :], axis=0, dtype=jnp.int32)
    padded = (counts + MOE_BLK - 1) // MOE_BLK * MOE_BLK
    pad_end = jnp.cumsum(padded)
    pad_start = pad_end - padded
    start = jnp.cumsum(counts) - counts
    nblk = -(-(n_pairs + N_EXPERTS * (MOE_BLK - 1)) // MOE_BLK)
    blk_row0 = jnp.arange(nblk, dtype=jnp.int32) * MOE_BLK
    block_e = jnp.minimum(jnp.sum(blk_row0[:, None] >= pad_end[None, :], axis=1, dtype=jnp.int32), N_EXPERTS - 1)
    blk_off = blk_row0 - pad_start[block_e]
    block_cnt = jnp.clip(counts[block_e] - blk_off, 0, MOE_BLK)
    lane = jnp.arange(MOE_BLK, dtype=jnp.int32)[None, :]
    valid = lane < block_cnt[:, None]
    src = jnp.where(valid, (start[block_e] + blk_off)[:, None] + lane, 0).reshape(-1)
    row_tgt = jnp.where(valid.reshape(-1), order[src], 0)
    row_p = jnp.where(valid.reshape(-1), probs[row_tgt], 0.0)
    xb = gather_rows(h, row_tgt // TOP_K, MOE_BLK)
    return moe_experts(xb, row_p.reshape(-1, 1), row_tgt, block_e, block_cnt, n_pairs,
                       layer, w_gu, b_gu, w_down, b_down)


def _split_groups(z):
    return z[:N_PROMPT].reshape(1, SEQ, -1), z[N_PROMPT:].reshape(DEC_BATCH, DEC_SEQ, -1)


def _join_groups(zp, zs):
    return jnp.concatenate([zp.reshape(N_PROMPT, -1), zs.reshape(N_SAMPLE, -1)], axis=0)


def layer(x, mod_g, lp, st, attn_tabs):
    d = D_MODEL
    h = norm_mod(x, lp['norm1_g'], mod_g, 0, 1)
    qkv = matmul(h, lp['w_qkv'], name="proj_qkv", tm=768, tn=512)
    pb = matmul(h, lp['w_pb'], name="proj_rwkv", tm=768, tn=1280)
    bch = matmul(h, lp['w_bch'], name="proj_sconv", tm=768, tn=512)
    pd = matmul(h, lp['w_pd'], name="proj_cconv", tm=768, tn=512)
    gates = matmul(h, lp['w_gl'], name="proj_gates", tm=768, tn=512, act="sigmoid")

    (bias_p, sink_p), (bias_s, sink_s) = attn_tabs
    o_a = jnp.concatenate([
        attn_prompt(qkv, bias_p, sink_p, N_PROMPT),
        attn_sample(qkv, st['k'].reshape(DEC_BATCH, WINDOW, A_KV), st['v'].reshape(DEC_BATCH, WINDOW, A_KV),
                    bias_s, sink_s, N_PROMPT)], axis=0)
    kp, ks = _split_groups(qkv[:, OFF_K:OFF_V])
    vp, vs = _split_groups(qkv[:, OFF_V:OFF_PB])
    new_kv_p = (kp[:, SEQ - WINDOW:].reshape(1, WINDOW, A_KV_HEADS, A_HEAD_DIM),
                vp[:, SEQ - WINDOW:].reshape(1, WINDOW, A_KV_HEADS, A_HEAD_DIM))
    new_kv_s = (ks.reshape(DEC_BATCH, DEC_SEQ, A_KV_HEADS, A_HEAD_DIM),
                vs.reshape(DEC_BATCH, DEC_SEQ, A_KV_HEADS, A_HEAD_DIM))

    pb_p, pb_s = _split_groups(pb)
    ob_p, shift_p, wkv_p = rwkv7_mix(pb_p, jnp.zeros((1, B_PROJ), F32),
                                     jnp.zeros((1, B_HEADS, B_HEAD, B_HEAD), F32), lp)
    ob_s, shift_s, wkv_s = rwkv7_mix(pb_s, st['shift'], st['wkv'], lp)
    o_b = _join_groups(ob_p, ob_s)

    ocp, sconv_p = short_conv(bch, jnp.zeros((1, C_WIDTH - 1, d), F32), lp['sconv_w'],
                              first_row=0, rows=CONV_ROWS)
    ocs, sconv_s = short_conv(bch, st['sconv'], lp['sconv_w'], first_row=N_PROMPT, rows=DEC_SEQ)
    o_c = jnp.concatenate([ocp, ocs], axis=0)
    cargs = (lp['cconv_w'], lp['cconv_b'], lp['cnorm_g'], lp['cnorm_b'])
    odp, cconv_p = conformer_conv(pd, jnp.zeros((1, D_WIDTH - 1, d), F32), *cargs,
                                  first_row=0, rows=CONV_ROWS)
    ods, cconv_s = conformer_conv(pd, st['cconv'], *cargs, first_row=N_PROMPT, rows=DEC_SEQ)
    o_d = jnp.concatenate([odp, ods], axis=0)

    merged = merge_branches([o_a, o_b.astype(BF16), o_c, o_d], lp['w_branch'], lp['layer'], gates)
    x = matmul_resid(merged, lp['w_out'], x, mod_g, 2)

    h2, logits = norm_mod(x, lp['norm2_g'], mod_g, 3, 4, router=(lp['router_w'], lp['router_b']))
    ffn = moe_ffn(h2, logits, lp['layer'], lp['w_gu'], lp['b_gu'], lp['w_down'], lp['b_down'])
    x = resid_gate(x, ffn, mod_g, 5)
    new_p = (new_kv_p[0], new_kv_p[1], wkv_p, shift_p, sconv_p, cconv_p)
    new_s = (new_kv_s[0], new_kv_s[1], wkv_s, shift_s, sconv_s, cconv_s)
    return x, new_p, new_s


def kernel(x_prompt, x_sample, c_prompt, c_sample, cache_k, cache_v, state_wkv, state_shift, state_sconv, state_cconv, w_mod, b_mod, norm1_g, norm2_g, w_in, attn_sinks, rel_bias_table, rwkv_mu, rwkv_w0, rwkv_w2, rwkv_a0, rwkv_a2, rwkv_g2, rwkv_k_k, rwkv_k_a, rwkv_r_k, rwkv_ln_g, rwkv_ln_b, sconv_w, cconv_w, cconv_b, cnorm_g, cnorm_b, w_branch, w_out, router_w, router_b, expert_w_gu, expert_b_gu, expert_w_down, expert_b_down, final_g):
    d = D_MODEL
    x = jnp.concatenate([x_prompt.reshape(N_PROMPT, d), x_sample.reshape(N_SAMPLE, d)], axis=0)
    n_seq = BATCH + DEC_BATCH
    c_act = jnp.pad(jnp.concatenate([jax.nn.silu(c_prompt), jax.nn.silu(c_sample)], axis=0),
                    ((0, 16 - n_seq), (0, 0)))

    blk_pos = jnp.arange(CHUNK, dtype=jnp.int32)
    qpos_p = WINDOW + blk_pos
    kpos_p = jnp.arange(WINDOW + CHUNK, dtype=jnp.int32)
    qpos_s = PAST_LEN + jnp.arange(DEC_SEQ, dtype=jnp.int32)
    kpos_s = PAST_LEN - WINDOW + jnp.arange(WINDOW + DEC_SEQ, dtype=jnp.int32)

    w_branch_bf = w_branch.astype(BF16)
    w_gu_bf = expert_w_gu.astype(BF16)
    w_down_bf = expert_w_down.astype(BF16)
    b_gu4 = expert_b_gu.reshape(DEPTH, N_EXPERTS, 1, -1)
    b_down4 = expert_b_down.reshape(DEPTH, N_EXPERTS, 1, -1)

    st_p, st_s = [], []
    for l in range(DEPTH):
        w_in_l = w_in[l]
        lp = dict(norm1_g=norm1_g[l], norm2_g=norm2_g[l],
                  w_qkv=w_in_l[:, OFF_Q:OFF_PB].astype(BF16), w_pb=w_in_l[:, OFF_PB:OFF_BG].astype(BF16),
                  w_bch=w_in_l[:, OFF_BG:OFF_PD].astype(BF16), w_pd=w_in_l[:, OFF_PD:OFF_GL].astype(BF16),
                  w_gl=w_in_l[:, OFF_GL:].astype(BF16),
                  mu=rwkv_mu[l], w0=rwkv_w0[l], w2=rwkv_w2[l], a0=rwkv_a0[l], a2=rwkv_a2[l],
                  g2=rwkv_g2[l], k_k=rwkv_k_k[l], k_a=rwkv_k_a[l], r_k=rwkv_r_k[l], ln_g=rwkv_ln_g[l],
                  ln_b=rwkv_ln_b[l], sconv_w=sconv_w[l], cconv_w=cconv_w[l], cconv_b=cconv_b[l],
                  cnorm_g=cnorm_g[l], cnorm_b=cnorm_b[l], layer=l, w_branch=w_branch_bf,
                  w_out=w_out[l].astype(BF16), router_w=router_w[l], router_b=router_b[l],
                  w_gu=w_gu_bf, b_gu=b_gu4, w_down=w_down_bf, b_down=b_down4)
        mod = matmul(c_act, w_mod[l].astype(BF16), name="modulation", tm=16, tn=512)[:n_seq] + b_mod[l]
        mod_g = jnp.concatenate([jnp.broadcast_to(mod[:1], (N_PROMPT // MOD_GROUP, 6 * d)),
                                 jnp.repeat(mod[1:], DEC_SEQ // MOD_GROUP, axis=0)], axis=0)
        tabs = (_attn_tables(qpos_p, kpos_p, rel_bias_table, attn_sinks[l], False),
                _attn_tables(qpos_s, kpos_s, rel_bias_table, attn_sinks[l], True))
        st = dict(k=cache_k[l], v=cache_v[l], shift=state_shift[l], wkv=state_wkv[l],
                  sconv=state_sconv[l], cconv=state_cconv[l])
        x, sp, ss = layer(x, mod_g, lp, st, tabs)
        st_p.append(sp)
        st_s.append(ss)
    y = final_norm(x, final_g)
    y_prompt = y[:N_PROMPT].reshape(1, SEQ, d)
    y_sample = y[N_PROMPT:].reshape(DEC_BATCH, DEC_SEQ, d)
    return (y_prompt, y_sample,
            jnp.stack([s[0] for s in st_p]), jnp.stack([s[1] for s in st_p]), jnp.stack([s[2] for s in st_p]),
            jnp.stack([s[3] for s in st_p]), jnp.stack([s[4] for s in st_p]), jnp.stack([s[5] for s in st_p]),
            jnp.stack([s[0] for s in st_s]), jnp.stack([s[1] for s in st_s]), jnp.stack([s[2] for s in st_s]),
            jnp.stack([s[3] for s in st_s]), jnp.stack([s[4] for s in st_s]), jnp.stack([s[5] for s in st_s]))
```

```python
import functools
import math

import jax
import jax.numpy as jnp
from jax import lax
from jax.experimental import pallas as pl
from jax.experimental.pallas import tpu as pltpu

F32 = jnp.float32
BF16 = jnp.bfloat16

D_MODEL = 2048
BATCH = 1
SEQ = 8192
DEPTH = 2
DEC_BATCH = 8
DEC_SEQ = 32
PAST_LEN = 1024
CHUNK = 64
NORM_EPS = 1e-5
NEG_INF = -1e30

A_HEAD_DIM = 64
A_HEADS = 32
A_KV_HEADS = 8
A_GROUP = 4
WINDOW = 128
WIN_CHUNKS = 2
NUM_BUCKETS = 32
MAX_DISTANCE = 128
A_Q = A_HEADS * A_HEAD_DIM
A_KV = A_KV_HEADS * A_HEAD_DIM

B_HEAD = 64
B_HEADS = 32
B_DIM = 2048
B_DECAY_RANK = 64
B_ICLR_RANK = 64
B_GATE_RANK = 128
B_GN_EPS = 64e-5
B_PROJ = 3 * B_DIM + B_DECAY_RANK + B_ICLR_RANK + B_GATE_RANK

C_WIDTH = 3
D_WIDTH = 31
N_BRANCH = 4

N_EXPERTS = 32
TOP_K = 4
D_FF = D_MODEL
SWIGLU_LIMIT = 7.0
SWIGLU_ALPHA = 1.702

OFF_Q = 0
OFF_K = OFF_Q + A_Q
OFF_V = OFF_K + A_KV
OFF_PB = OFF_V + A_KV
OFF_BG = OFF_PB + B_PROJ
OFF_CG = OFF_BG + D_MODEL
OFF_HC = OFF_CG + D_MODEL
OFF_PD = OFF_HC + D_MODEL
OFF_GL = OFF_PD + 2 * D_MODEL
IN_TOTAL = OFF_GL + N_BRANCH * D_MODEL

N_PROMPT = SEQ
N_SAMPLE = DEC_BATCH * DEC_SEQ
N_TOK = N_PROMPT + N_SAMPLE
MOD_GROUP = 32
N_MOD_GROUPS = N_TOK // MOD_GROUP

LANES = 128
RWKV_T = 64
RWKV_PAIRS = 16
RWKV_SEQ_T = 16
RWKV_TAIL = 256
CONV_ROWS = 256
CONV_STRIP = 512
CONV_C_HALO = 8
CONV_D_HALO = 32
MOE_BLK = 256
MOE_FF_TILE = 512
DMA_UNROLL = 8
VMEM_LIMIT = 48 * 1024 * 1024
MOE_VMEM_LIMIT = 56 * 1024 * 1024


def _cparams(sem):
    return pltpu.CompilerParams(dimension_semantics=sem, vmem_limit_bytes=VMEM_LIMIT)


def _dot(a, b):
    return jnp.dot(a, b, preferred_element_type=F32)


def _dot_nt(a, b):
    return lax.dot_general(a, b, (((1,), (1,)), ((), ())), preferred_element_type=F32)


def _mm_kernel(x_ref, w_ref, o_ref, *, act):
    y = _dot(x_ref[...].astype(BF16), w_ref[...])
    if act == "sigmoid":
        y = jax.nn.sigmoid(y)
    o_ref[...] = y.astype(o_ref.dtype)


def matmul(x, w, *, name, tm, tn, out_dtype=F32, act=None):
    m, k = x.shape
    n = w.shape[1]
    assert m % tm == 0 and n % tn == 0
    return pl.pallas_call(
        functools.partial(_mm_kernel, act=act), name=name,
        grid=(m // tm, n // tn),
        in_specs=[pl.BlockSpec((tm, k), lambda i, j: (i, 0)),
                  pl.BlockSpec((k, tn), lambda i, j: (0, j))],
        out_specs=pl.BlockSpec((tm, tn), lambda i, j: (i, j)),
        out_shape=jax.ShapeDtypeStruct((m, n), out_dtype),
        compiler_params=_cparams(("parallel", "parallel")),
    )(x, w)


def _group_rows(gi):
    return slice(gi * MOD_GROUP, (gi + 1) * MOD_GROUP)


def _norm_mod_rows(x_ref, g_ref, sh_ref, sc_ref, gi):
    x = x_ref[_group_rows(gi), :]
    y = x * lax.rsqrt(jnp.mean(x * x, axis=-1, keepdims=True) + NORM_EPS) * g_ref[...]
    return y * (1.0 + sc_ref[gi:gi + 1, :]) + sh_ref[gi:gi + 1, :]


def _norm_mod_kernel(x_ref, g_ref, sh_ref, sc_ref, h_ref):
    for gi in range(x_ref.shape[0] // MOD_GROUP):
        h_ref[_group_rows(gi), :] = _norm_mod_rows(x_ref, g_ref, sh_ref, sc_ref, gi).astype(h_ref.dtype)


def _norm_mod_router_kernel(x_ref, g_ref, sh_ref, sc_ref, rw_ref, rb_ref, h_ref, lg_ref):
    w = rw_ref[...].astype(BF16)
    for gi in range(x_ref.shape[0] // MOD_GROUP):
        h = _norm_mod_rows(x_ref, g_ref, sh_ref, sc_ref, gi).astype(BF16)
        h_ref[_group_rows(gi), :] = h.astype(h_ref.dtype)
        lg_ref[_group_rows(gi), :] = _dot(h, w) + rb_ref[...]


def norm_mod(x, g, mod_g, shift_idx, scale_idx, router=None, tm=256):
    m, d = x.shape
    gm = tm // MOD_GROUP
    in_specs = [pl.BlockSpec((tm, d), lambda i: (i, 0)),
                pl.BlockSpec((1, d), lambda i: (0, 0)),
                pl.BlockSpec((gm, d), lambda i: (i, shift_idx)),
                pl.BlockSpec((gm, d), lambda i: (i, scale_idx))]
    args = [x, g.reshape(1, d), mod_g, mod_g]
    if router is None:
        return pl.pallas_call(
            _norm_mod_kernel, name="norm_mod", grid=(m // tm,), in_specs=in_specs,
            out_specs=pl.BlockSpec((tm, d), lambda i: (i, 0)),
            out_shape=jax.ShapeDtypeStruct((m, d), BF16),
            compiler_params=_cparams(("parallel",)))(*args)
    rw, rb = router
    ne = rw.shape[1]
    in_specs += [pl.BlockSpec((d, ne), lambda i: (0, 0)), pl.BlockSpec((1, ne), lambda i: (0, 0))]
    args += [rw, rb.reshape(1, ne)]
    return pl.pallas_call(
        _norm_mod_router_kernel, name="norm_router", grid=(m // tm,), in_specs=in_specs,
        out_specs=[pl.BlockSpec((tm, d), lambda i: (i, 0)), pl.BlockSpec((tm, ne), lambda i: (i, 0))],
        out_shape=[jax.ShapeDtypeStruct((m, d), F32), jax.ShapeDtypeStruct((m, ne), F32)],
        compiler_params=_cparams(("parallel",)))(*args)


def _final_norm_kernel(x_ref, g_ref, o_ref):
    x = x_ref[...]
    o_ref[...] = x * lax.rsqrt(jnp.mean(x * x, axis=-1, keepdims=True) + NORM_EPS) * g_ref[...]


def final_norm(x, g, tm=256):
    m, d = x.shape
    return pl.pallas_call(
        _final_norm_kernel, name="final_norm", grid=(m // tm,),
        in_specs=[pl.BlockSpec((tm, d), lambda i: (i, 0)), pl.BlockSpec((1, d), lambda i: (0, 0))],
        out_specs=pl.BlockSpec((tm, d), lambda i: (i, 0)),
        out_shape=jax.ShapeDtypeStruct((m, d), F32),
        compiler_params=_cparams(("parallel",)))(x, g.reshape(1, d))


def _mm_resid_kernel(a_ref, w_ref, x_ref, gate_ref, o_ref):
    y = _dot(a_ref[...], w_ref[...])
    for gi in range(y.shape[0] // MOD_GROUP):
        rows = _group_rows(gi)
        o_ref[rows, :] = x_ref[rows, :] + gate_ref[gi:gi + 1, :] * y[rows]


def matmul_resid(a, w, x, mod_g, gate_idx, *, tm=768, tn=512):
    m, k = a.shape
    n = w.shape[1]
    gm = tm // MOD_GROUP
    nj = n // tn
    return pl.pallas_call(
        _mm_resid_kernel, name="out_proj", grid=(m // tm, nj),
        in_specs=[pl.BlockSpec((tm, k), lambda i, j: (i, 0)),
                  pl.BlockSpec((k, tn), lambda i, j: (0, j)),
                  pl.BlockSpec((tm, tn), lambda i, j: (i, j)),
                  pl.BlockSpec((gm, tn), lambda i, j: (i, gate_idx * nj + j))],
        out_specs=pl.BlockSpec((tm, tn), lambda i, j: (i, j)),
        out_shape=jax.ShapeDtypeStruct((m, n), F32),
        compiler_params=_cparams(("parallel", "parallel")))(a, w, x, mod_g)


def _resid_kernel(x_ref, gate_ref, *refs):
    f_refs, o_ref = refs[:TOP_K], refs[TOP_K]
    for gi in range(x_ref.shape[0] // MOD_GROUP):
        rows = _group_rows(gi)
        ffn = f_refs[0][rows, :]
        for k in range(1, TOP_K):
            ffn = ffn + f_refs[k][rows, :]
        o_ref[rows, :] = x_ref[rows, :] + gate_ref[gi:gi + 1, :] * ffn


def resid_gate(x, f_pairs, mod_g, gate_idx, tm=256):
    m, d = x.shape
    gm = tm // MOD_GROUP
    nt = m // tm
    fspec = lambda k: pl.BlockSpec((tm, d), lambda i: (k * nt + i, 0))
    return pl.pallas_call(
        _resid_kernel, name="ffn_resid", grid=(nt,),
        in_specs=[pl.BlockSpec((tm, d), lambda i: (i, 0)), pl.BlockSpec((gm, d), lambda i: (i, gate_idx))]
                 + [fspec(k) for k in range(TOP_K)],
        out_specs=pl.BlockSpec((tm, d), lambda i: (i, 0)),
        out_shape=jax.ShapeDtypeStruct((m, d), F32),
        compiler_params=_cparams(("parallel",)))(x, mod_g, *([f_pairs] * TOP_K))


def _merge_kernel(ba_ref, bb_ref, bc_ref, bd_ref, w_ref, ga_ref, gb_ref, gc_ref, gd_ref, o_ref):
    acc = None
    for n, (b_ref, g_ref) in enumerate(((ba_ref, ga_ref), (bb_ref, gb_ref), (bc_ref, gc_ref), (bd_ref, gd_ref))):
        part = g_ref[...] * _dot(b_ref[...], w_ref[0, n])
        acc = part if acc is None else acc + part
    o_ref[...] = acc.astype(o_ref.dtype)


def merge_branches(branches, w_branch, layer, gates, *, tm=768, tn=256):
    m, k = branches[0].shape
    d = w_branch.shape[3]
    nj = d // tn
    bspec = pl.BlockSpec((tm, k), lambda i, j: (i, 0))
    gspec = lambda n: pl.BlockSpec((tm, tn), lambda i, j: (i, n * nj + j))
    return pl.pallas_call(
        _merge_kernel, name="merge", grid=(m // tm, nj),
        in_specs=[bspec] * N_BRANCH + [pl.BlockSpec((1, N_BRANCH, k, tn), lambda i, j: (layer, 0, 0, j))]
                 + [gspec(n) for n in range(N_BRANCH)],
        out_specs=pl.BlockSpec((tm, tn), lambda i, j: (i, j)),
        out_shape=jax.ShapeDtypeStruct((m, d), BF16),
        compiler_params=_cparams(("parallel", "parallel")))(*branches, w_branch, gates, gates, gates, gates)


def _attn_core(q, k, v, bias_ref, sink_ref, o_ref, key_ok):
    sq = q.shape[0]
    lane = lax.broadcasted_iota(jnp.int32, (1, LANES), 1)
    low = lane < A_HEAD_DIM
    q = q * (A_HEAD_DIM ** -0.5)
    groups = range(A_KV_HEADS)
    k2, v2 = [], []
    for kvp in range(A_KV_HEADS // 2):
        kp = k[:, kvp * LANES:(kvp + 1) * LANES]
        vp = v[:, kvp * LANES:(kvp + 1) * LANES]
        kp_sw = pltpu.roll(kp, A_HEAD_DIM, 1)
        vp_sw = pltpu.roll(vp, A_HEAD_DIM, 1)
        k2 += [jnp.where(low, kp, kp_sw).astype(BF16), jnp.where(low, kp_sw, kp).astype(BF16)]
        v2 += [jnp.where(low, vp, vp_sw).astype(BF16), jnp.where(low, vp_sw, vp).astype(BF16)]
    lhs = []
    for g in groups:
        parts = []
        for j in range(2):
            qp = q[:, (2 * g + j) * LANES:(2 * g + j + 1) * LANES]
            parts.append(jnp.where(low, qp, 0.0))
            parts.append(jnp.where(low, 0.0, qp))
        lhs.append(jnp.concatenate(parts, axis=0).astype(BF16))
    s = [_dot_nt(lhs[g], k2[g]) + bias_ref[g] for g in groups]
    if key_ok is not None:
        s = [jnp.where(key_ok, z, NEG_INF) for z in s]
    sink = [sink_ref[g] for g in groups]
    m = [jnp.maximum(jnp.max(s[g], axis=-1, keepdims=True), sink[g]) for g in groups]
    p = [jnp.exp(s[g] - m[g]) for g in groups]
    p = [p[g] / (jnp.sum(p[g], axis=-1, keepdims=True) + jnp.exp(sink[g] - m[g])) for g in groups]
    o2 = [_dot(p[g].astype(BF16), v2[g]) for g in groups]
    for g in groups:
        for j in range(2):
            oj = jnp.where(low, o2[g][(2 * j) * sq:(2 * j + 1) * sq], o2[g][(2 * j + 1) * sq:(2 * j + 2) * sq])
            o_ref[:, (2 * g + j) * LANES:(2 * g + j + 1) * LANES] = oj.astype(o_ref.dtype)


def _attn_prompt_kernel(q_ref, k0_ref, k1_ref, k2_ref, v0_ref, v1_ref, v2_ref, bias_ref, sink_ref, o_ref):
    c = pl.program_id(0)
    k = jnp.concatenate([k0_ref[...], k1_ref[...], k2_ref[...]], axis=0)
    v = jnp.concatenate([v0_ref[...], v1_ref[...], v2_ref[...]], axis=0)
    sk = k.shape[0]
    k_pos = (c - WIN_CHUNKS) * CHUNK + lax.broadcasted_iota(jnp.int32, (1, sk), 1)
    _attn_core(q_ref[...], k, v, bias_ref, sink_ref, o_ref, k_pos >= 0)


_QKV_KCOL = A_Q // A_KV
_QKV_VCOL = _QKV_KCOL + 1


def attn_prompt(qkv, bias, sink_col, n_prompt):
    nc = n_prompt // CHUNK
    kspec = lambda d, col: pl.BlockSpec((CHUNK, A_KV), lambda c: (jnp.maximum(c - d, 0), col))
    return pl.pallas_call(
        _attn_prompt_kernel, name="attn_prompt", grid=(nc,),
        in_specs=[pl.BlockSpec((CHUNK, A_Q), lambda c: (c, 0)),
                  kspec(2, _QKV_KCOL), kspec(1, _QKV_KCOL), kspec(0, _QKV_KCOL),
                  kspec(2, _QKV_VCOL), kspec(1, _QKV_VCOL), kspec(0, _QKV_VCOL),
                  pl.BlockSpec(bias.shape, lambda c: (0, 0, 0)),
                  pl.BlockSpec(sink_col.shape, lambda c: (0, 0, 0))],
        out_specs=pl.BlockSpec((CHUNK, A_Q), lambda c: (c, 0)),
        out_shape=jax.ShapeDtypeStruct((n_prompt, A_Q), BF16),
        compiler_params=_cparams(("parallel",)))(qkv, qkv, qkv, qkv, qkv, qkv, qkv, bias, sink_col)


def _attn_sample_kernel(q_ref, kn_ref, vn_ref, kc_ref, vc_ref, bias_ref, sink_ref, o_ref):
    k = jnp.concatenate([kc_ref[0], kn_ref[...]], axis=0)
    v = jnp.concatenate([vc_ref[0], vn_ref[...]], axis=0)
    _attn_core(q_ref[...], k, v, bias_ref, sink_ref, o_ref, None)


def attn_sample(qkv, k_cache, v_cache, bias, sink_col, n_prompt):
    nb = k_cache.shape[0]
    base = n_prompt // DEC_SEQ
    return pl.pallas_call(
        _attn_sample_kernel, name="attn_sample", grid=(nb,),
        in_specs=[pl.BlockSpec((DEC_SEQ, A_Q), lambda b: (base + b, 0)),
                  pl.BlockSpec((DEC_SEQ, A_KV), lambda b: (base + b, _QKV_KCOL)),
                  pl.BlockSpec((DEC_SEQ, A_KV), lambda b: (base + b, _QKV_VCOL)),
                  pl.BlockSpec((1, WINDOW, A_KV), lambda b: (b, 0, 0)),
                  pl.BlockSpec((1, WINDOW, A_KV), lambda b: (b, 0, 0)),
                  pl.BlockSpec(bias.shape, lambda b: (0, 0, 0)),
                  pl.BlockSpec(sink_col.shape, lambda b: (0, 0, 0))],
        out_specs=pl.BlockSpec((DEC_SEQ, A_Q), lambda b: (b, 0)),
        out_shape=jax.ShapeDtypeStruct((nb * DEC_SEQ, A_Q), BF16),
        compiler_params=_cparams(("parallel",)))(qkv, qkv, qkv, k_cache, v_cache, bias, sink_col)


def _t5_bucket(rel):
    nb = NUM_BUCKETS // 2
    max_exact = nb // 2
    ret = jnp.where(rel > 0, nb, 0)
    n = jnp.abs(rel)
    nf = jnp.maximum(n, 1).astype(F32)
    large = max_exact + (jnp.log(nf / max_exact) / math.log(MAX_DISTANCE / max_exact) * (nb - max_exact)).astype(jnp.int32)
    large = jnp.minimum(large, nb - 1)
    return ret + jnp.where(n < max_exact, n, large)


def _attn_tables(q_pos, k_pos, table, sinks, mask_positions):
    sq, sk = q_pos.shape[0], k_pos.shape[0]
    b = table[_t5_bucket(k_pos[None, :] - q_pos[:, None])]
    b = jnp.transpose(b, (2, 0, 1)).astype(F32)
    if mask_positions:
        qc, kc = q_pos[:, None] // CHUNK, k_pos[None, :] // CHUNK
        vis = (k_pos[None, :] >= 0) & (kc <= qc) & (kc >= qc - WIN_CHUNKS)
        b = jnp.where(vis[None], b, NEG_INF)
    bias = b.reshape(A_KV_HEADS, A_GROUP * sq, sk)
    sink_col = jnp.broadcast_to(sinks.astype(F32).reshape(A_KV_HEADS, A_GROUP, 1, 1),
                                (A_KV_HEADS, A_GROUP, sq, 1)).reshape(A_KV_HEADS, A_GROUP * sq, 1)
    return bias, sink_col


def _rwkv_kernel(r_ref, lw_ref, k_ref, v_ref, kk_ref, a_ref, s0_ref, y_ref, st_ref, s_scr):
    c = pl.program_id(2)
    t = RWKV_T

    @pl.when(c == 0)
    def _():
        s_scr[...] = s0_ref[0]

    row = lax.broadcasted_iota(jnp.int32, (t, LANES), 0)
    lane = lax.broadcasted_iota(jnp.int32, (t, LANES), 1)
    low = lane < B_HEAD
    col = lane & (t - 1)
    strict = col < row
    incl = col <= row
    tri = jnp.where(lax.broadcasted_iota(jnp.int32, (t, t), 1) <= lax.broadcasted_iota(jnp.int32, (t, t), 0),
                    1.0, 0.0).astype(BF16)
    r128 = lax.broadcasted_iota(jnp.int32, (LANES, LANES), 0)
    c128 = lax.broadcasted_iota(jnp.int32, (LANES, LANES), 1)
    same_head = (r128 < B_HEAD) == (c128 < B_HEAD)

    def stack(x):
        return jnp.concatenate([jnp.where(low, x, 0.0), jnp.where(low, 0.0, x)], axis=0)

    pairs = range(RWKV_PAIRS)
    sls = [slice(p * LANES, (p + 1) * LANES) for p in pairs]
    n_steps = int(math.log2(t))

    def prepare(sl):
        lw = lw_ref[0, :, sl]
        hi = lw.astype(BF16)
        rem = lw - hi.astype(F32)
        mid = rem.astype(BF16)
        lo = (rem - mid.astype(F32)).astype(BF16)
        cs = _dot(tri, hi) + _dot(tri, mid) + _dot(tri, lo)
        cs_end = cs[t - 1:t, :]
        kk = kk_ref[0, :, sl]
        ka = kk * a_ref[0, :, sl]
        kx = k_ref[0, :, sl]
        g_inv = jnp.exp(-cs)
        g_end = jnp.exp(cs_end - cs)
        a_t = -kk * jnp.exp(cs - lw)
        r_t = r_ref[0, :, sl] * jnp.exp(cs)
        lhs = jnp.concatenate([a_t, r_t], axis=0).astype(BF16)
        rhs = jnp.concatenate([stack(ka * g_inv), stack(kx * g_inv)], axis=0).astype(BF16)
        tail = jnp.concatenate([ka * g_end, kx * g_end], axis=0).astype(BF16)
        return lhs, rhs, tail, jnp.exp(cs_end)

    prep = [prepare(sl) for sl in sls]
    vs = [v_ref[0, :, sl] for sl in sls]
    st_v = [stack(v).astype(BF16) for v in vs]
    ss = [s_scr[p] for p in pairs]
    d1 = [_dot_nt(prep[p][0], prep[p][1]) for p in pairs]
    d2 = [_dot_nt(prep[p][0], ss[p].astype(BF16)) for p in pairs]
    l_ak = [jnp.where(strict, d1[p][:t, 2 * t:], 0.0).astype(BF16) for p in pairs]
    pw = [jnp.where(strict, d1[p][:t, :2 * t], 0.0) for p in pairs]
    l_r = [jnp.concatenate([jnp.where(incl, d1[p][t:, :2 * t], 0.0), jnp.where(incl, d1[p][t:, 2 * t:], 0.0)],
                           axis=1).astype(BF16) for p in pairs]
    u = [d2[p][:t] + _dot(l_ak[p], st_v[p]) for p in pairs]
    for i in range(n_steps):
        u = [u[p] + _dot(pw[p].astype(BF16), stack(u[p]).astype(BF16)) for p in pairs]
        if i < n_steps - 1:
            pw = [_dot(pw[p].astype(BF16), stack(pw[p]).astype(BF16)) for p in pairs]
    y = [d2[p][t:] + _dot(l_r[p], jnp.concatenate([stack(u[p]).astype(BF16), st_v[p]], axis=0)) for p in pairs]
    upd = [_dot(jnp.concatenate([u[p], vs[p]], axis=0).T.astype(BF16), prep[p][2]) for p in pairs]
    for p in pairs:
        s_scr[p] = ss[p] * prep[p][3] + jnp.where(same_head, upd[p], 0.0)
        y_ref[0, :, sls[p]] = y[p]

    @pl.when(c == pl.num_programs(2) - 1)
    def _():
        st_ref[0] = s_scr[...]


def rwkv_scan(r, lw, k, v, kk, a, s0, l):
    b, _, d = r.shape
    npair = d // LANES
    ng = npair // RWKV_PAIRS
    w = RWKV_PAIRS * LANES
    seq = pl.BlockSpec((1, RWKV_T, w), lambda bi, gi, ci: (bi, ci, gi))
    sspec = pl.BlockSpec((1, RWKV_PAIRS, LANES, LANES), lambda bi, gi, ci: (bi, gi, 0, 0))
    return pl.pallas_call(
        _rwkv_kernel, name="rwkv_chunk", grid=(b, ng, l // RWKV_T),
        in_specs=[seq] * 6 + [sspec],
        out_specs=[seq, sspec],
        out_shape=[jax.ShapeDtypeStruct((b, l, d), F32), jax.ShapeDtypeStruct(s0.shape, F32)],
        scratch_shapes=[pltpu.VMEM((RWKV_PAIRS, LANES, LANES), F32)],
        compiler_params=_cparams(("parallel", "parallel", "arbitrary")))(r, lw, k, v, kk, a, s0)


def _rwkv_seq_kernel(r_ref, w_ref, k_ref, v_ref, kk_ref, a_ref, s0_ref, y_ref, st_ref, s_scr, sb_scr, vk_scr):
    c = pl.program_id(1)
    t_len = r_ref.shape[1]
    npair = r_ref.shape[2]
    n = B_HEAD

    @pl.when(c == 0)
    def _():
        s_scr[...] = s0_ref[0]
        sb_scr[...] = s0_ref[0].astype(BF16)

    low = lax.broadcasted_iota(jnp.int32, (1, LANES), 1) < n
    eye = jnp.where((lax.broadcasted_iota(jnp.int32, (n, LANES), 1) & (n - 1))
                    == lax.broadcasted_iota(jnp.int32, (n, LANES), 0), 1.0, 0.0).astype(BF16)
    eye3 = jnp.concatenate([eye, eye, eye], axis=1)

    def per_head_rows(x, rows):
        x0 = jnp.where(low, x, 0.0).astype(BF16)
        x1 = jnp.where(low, 0.0, x).astype(BF16)
        return jnp.concatenate([jnp.broadcast_to(x0, (rows, LANES)), jnp.broadcast_to(x1, (rows, LANES))], axis=0)

    def outer(t, carry):
        v_all, k_all = v_ref[0, t], k_ref[0, t]
        hi = v_all.astype(BF16).astype(F32)
        mid = (v_all - hi).astype(BF16).astype(F32)
        lo = v_all - hi - mid
        wmats = [jnp.concatenate([per_head_rows(z[p:p + 1], n) for z in (hi, mid, lo)], axis=1) for p in range(npair)]
        vcols = [_dot_nt(eye3, wmat) for wmat in wmats]
        for p in range(npair):
            vk_scr[p, t] = vcols[p] * k_all[p:p + 1]
        return carry

    lax.fori_loop(0, t_len, outer, 0)

    def step(t, carry):
        kk_all, a_all, w_all, r_all = kk_ref[0, t], a_ref[0, t], w_ref[0, t], r_ref[0, t]
        ka_all = kk_all * a_all
        pairs = range(npair)
        m_kk = [per_head_rows(-kk_all[p:p + 1], n) for p in pairs]
        m_r = [per_head_rows(r_all[p:p + 1], 8) for p in pairs]
        sa = [_dot_nt(sb_scr[p], m_kk[p]) for p in pairs]
        s = [s_scr[p] * w_all[p:p + 1] + sa[p] * ka_all[p:p + 1] + vk_scr[p, t] for p in pairs]
        sb = [z.astype(BF16) for z in s]
        for p in pairs:
            s_scr[p] = s[p]
            sb_scr[p] = sb[p]
        yy = [_dot_nt(m_r[p], jnp.concatenate([sb[p], sb[p]], axis=0)) for p in pairs]
        for p in pairs:
            y_ref[0, t, p:p + 1, :] = jnp.where(low, yy[p][0:1], yy[p][8:9])
        return carry

    lax.fori_loop(0, t_len, step, 0)

    @pl.when(c == pl.num_programs(1) - 1)
    def _():
        st_ref[0] = s_scr[...]


def rwkv_seq(r, w, k, v, kk, a, s0, t_blk=RWKV_SEQ_T):
    b, l, d = r.shape
    npair = d // LANES
    t_blk = min(t_blk, l)
    assert l % t_blk == 0
    seq = pl.BlockSpec((1, t_blk, npair, LANES), lambda bi, ci: (bi, ci, 0, 0))
    sspec = pl.BlockSpec((1, npair, B_HEAD, LANES), lambda bi, ci: (bi, 0, 0, 0))
    y, s_new = pl.pallas_call(
        _rwkv_seq_kernel, name="rwkv_seq", grid=(b, l // t_blk),
        in_specs=[seq] * 6 + [sspec],
        out_specs=[seq, sspec],
        out_shape=[jax.ShapeDtypeStruct((b, l, npair, LANES), F32), jax.ShapeDtypeStruct(s0.shape, F32)],
        scratch_shapes=[pltpu.VMEM((npair, B_HEAD, LANES), F32), pltpu.VMEM((npair, B_HEAD, LANES), BF16),
                        pltpu.VMEM((npair, t_blk, B_HEAD, LANES), F32)],
        compiler_params=_cparams(("arbitrary", "arbitrary")))(
            *[z.reshape(b, l, npair, LANES) for z in (r, w, k, v, kk, a)], s0)
    return y.reshape(b, l, d), s_new


def _pair_states(s):
    b = s.shape[0]
    s = s.reshape(b, B_HEADS // 2, 2, B_HEAD, B_HEAD)
    return jnp.swapaxes(s, 2, 3).reshape(b, B_HEADS // 2, B_HEAD, 2 * B_HEAD)


def _unpair_states(sp):
    b = sp.shape[0]
    s = sp.reshape(b, B_HEADS // 2, B_HEAD, 2, B_HEAD)
    return jnp.swapaxes(s, 2, 3).reshape(b, B_HEADS, B_HEAD, B_HEAD)


def _pack_states(s):
    b = s.shape[0]
    s = s.reshape(b, B_HEADS // 2, 2, B_HEAD, B_HEAD)
    z = jnp.zeros_like(s[:, :, 0])
    top = jnp.concatenate([s[:, :, 0], z], axis=-1)
    bot = jnp.concatenate([z, s[:, :, 1]], axis=-1)
    return jnp.concatenate([top, bot], axis=-2)


def _unpack_states(sp):
    b = sp.shape[0]
    s0 = sp[:, :, :B_HEAD, :B_HEAD]
    s1 = sp[:, :, B_HEAD:, B_HEAD:]
    return jnp.stack([s0, s1], axis=2).reshape(b, B_HEADS, B_HEAD, B_HEAD)


def rwkv7_mix(pb, shift_prev, wkv_prev, lp):
    b, t, _ = pb.shape
    prev = jnp.concatenate([shift_prev[:, None], pb[:, :-1]], axis=1)
    xm = pb + (prev - pb) * lp['mu']
    splits = [B_DIM, 2 * B_DIM, 3 * B_DIM, 3 * B_DIM + B_DECAY_RANK, 3 * B_DIM + B_DECAY_RANK + B_ICLR_RANK]
    r, k, v, wl, al, gl = jnp.split(xm, splits, axis=-1)
    logw = -jax.nn.softplus(-(lp['w0'] + jnp.tanh(wl) @ lp['w2'])) - 0.5
    lw = -jnp.exp(logw)
    a = jax.nn.sigmoid(lp['a0'] + al @ lp['a2'])
    g = jax.nn.sigmoid(gl) @ lp['g2']
    heads = lambda z: z.reshape(b, t, B_HEADS, B_HEAD)
    kkh = heads(k * lp['k_k'])
    kkh = kkh * lax.rsqrt(jnp.sum(kkh * kkh, axis=-1, keepdims=True) + 1e-12)
    kk = kkh.reshape(b, t, B_DIM)
    k2 = k * (1.0 + (a - 1.0) * lp['k_a'])
    t_head = max(t - RWKV_TAIL, 0) // RWKV_T * RWKV_T
    seqs = [r, lw, k2, v, kk, a]
    state = wkv_prev
    ys = []
    if t_head:
        y_head, s_bd = rwkv_scan(*seqs, _pack_states(state), t_head)
        state = _unpack_states(s_bd)
        ys.append(y_head)
    seqs[1] = jnp.exp(lw)
    y_tail, s_pair = rwkv_seq(*[z[:, t_head:] for z in seqs], _pair_states(state))
    ys.append(y_tail)
    y = heads(jnp.concatenate(ys, axis=1))
    mu = jnp.mean(y, axis=-1, keepdims=True)
    var = jnp.mean(jnp.square(y - mu), axis=-1, keepdims=True)
    yn = ((y - mu) * lax.rsqrt(var + B_GN_EPS)).reshape(b, t, B_DIM) * lp['ln_g'] + lp['ln_b']
    bonus = jnp.sum(heads(r) * heads(k2) * lp['r_k'], axis=-1, keepdims=True) * heads(v)
    o = (yn + bonus.reshape(b, t, B_DIM)) * g
    return o, pb[:, -1], _unpair_states(s_pair)


def _round_bf16(x):
    return x.astype(BF16).astype(F32)


def _conv_stage(u, st_ref, ns_ref, xbuf, *, halo):
    i = pl.program_id(1)
    rows = u.shape[0]

    @pl.when(i == 0)
    def _():
        xbuf[0:halo, :] = _round_bf16(st_ref[0])

    @pl.when(i > 0)
    def _():
        xbuf[0:halo, :] = xbuf[rows:rows + halo, :]

    xbuf[halo:halo + rows, :] = _round_bf16(u)

    @pl.when(i == pl.num_programs(1) - 1)
    def _():
        ns_ref[0] = u[rows - halo:, :]


def _conv_taps(xbuf, w_ref, rows, cols, *, width, halo):
    acc = None
    for j in range(width):
        term = xbuf[pl.ds(halo - (width - 1) + j, rows), cols] * _round_bf16(w_ref[j:j + 1, cols])
        acc = term if acc is None else acc + term
    return acc


def _sconv_kernel(bg_ref, cg_ref, hc_ref, st_ref, w_ref, o_ref, ns_ref, xbuf):
    rows = o_ref.shape[0]
    _conv_stage(cg_ref[...] * hc_ref[...], st_ref, ns_ref, xbuf, halo=CONV_C_HALO)
    for c0 in range(0, o_ref.shape[1], CONV_STRIP):
        cols = slice(c0, c0 + CONV_STRIP)
        y = _conv_taps(xbuf, w_ref, rows, cols, width=C_WIDTH, halo=CONV_C_HALO)
        o_ref[:, cols] = (bg_ref[:, cols] * y).astype(o_ref.dtype)


def _cconv_kernel(ga_ref, gb_ref, st_ref, w_ref, b_ref, g_ref, beta_ref, o_ref, ns_ref, xbuf, ybuf):
    rows = o_ref.shape[0]
    _conv_stage(ga_ref[...] * jax.nn.sigmoid(gb_ref[...]), st_ref, ns_ref, xbuf,
                halo=CONV_D_HALO)
    for c0 in range(0, o_ref.shape[1], CONV_STRIP):
        cols = slice(c0, c0 + CONV_STRIP)
        ybuf[:, cols] = _conv_taps(xbuf, w_ref, rows, cols, width=D_WIDTH, halo=CONV_D_HALO) + b_ref[:, cols]
    z = ybuf[...]
    mu = jnp.mean(z, axis=-1, keepdims=True)
    var = jnp.mean(jnp.square(z - mu), axis=-1, keepdims=True)
    zn = (z - mu) * lax.rsqrt(var + NORM_EPS) * g_ref[...] + beta_ref[...]
    o_ref[...] = (zn * jax.nn.sigmoid(zn)).astype(o_ref.dtype)


def _conv_call(kernel_fn, name, srcs, state, consts, *, width, halo, first_row, rows, extra_scratch):
    nseq = state.shape[0]
    d = D_MODEL
    hist = width - 1
    seq_len = N_PROMPT if nseq == 1 else DEC_SEQ
    assert seq_len % rows == 0 and rows >= halo >= hist
    tiles = seq_len // rows
    base = first_row // rows
    row_blk = lambda b, i: base + b * tiles + i
    in_specs = [pl.BlockSpec((rows, d), functools.partial(lambda b, i, c: (row_blk(b, i), c), c=col))
                for _, col in srcs]
    in_specs.append(pl.BlockSpec((1, halo, d), lambda b, i: (b, 0, 0)))
    in_specs += [pl.BlockSpec(c.shape, lambda b, i: (0, 0)) for c in consts]
    state_pad = jnp.pad(state, ((0, 0), (halo - hist, 0), (0, 0)))
    out, tail = pl.pallas_call(
        functools.partial(kernel_fn), name=name, grid=(nseq, tiles),
        in_specs=in_specs,
        out_specs=[pl.BlockSpec((rows, d), lambda b, i: (b * tiles + i, 0)),
                   pl.BlockSpec((1, halo, d), lambda b, i: (b, 0, 0))],
        out_shape=[jax.ShapeDtypeStruct((nseq * seq_len, d), BF16), jax.ShapeDtypeStruct((nseq, halo, d), F32)],
        scratch_shapes=[pltpu.VMEM((halo + rows, d), F32)] + extra_scratch(rows),
        compiler_params=_cparams(("parallel", "arbitrary")))(*[a for a, _ in srcs], state_pad, *consts)
    return out, tail[:, halo - hist:]


def short_conv(bch, state, w, *, first_row, rows):
    return _conv_call(_sconv_kernel, "sconv", [(bch, 0), (bch, 1), (bch, 2)], state, [w],
                      width=C_WIDTH, halo=CONV_C_HALO, first_row=first_row, rows=rows,
                      extra_scratch=lambda r: [])


def conformer_conv(pd, state, w, b, g, beta, *, first_row, rows):
    row = lambda z: z.reshape(1, D_MODEL)
    return _conv_call(_cconv_kernel, "cconv", [(pd, 0), (pd, 1)], state, [w, row(b), row(g), row(beta)],
                      width=D_WIDTH, halo=CONV_D_HALO, first_row=first_row, rows=rows,
                      extra_scratch=lambda r: [pltpu.VMEM((r, D_MODEL), F32)])


def _moe_up_kernel(be_ref, cnt_ref, x_ref, w_ref, b_ref, o_ref):
    i = pl.program_id(0)

    @pl.when(cnt_ref[i] > 0)
    def _():
        ff = o_ref.shape[1]
        tf = min(ff, MOE_FF_TILE)
        x = x_ref[...].astype(BF16)
        for f in range(ff // tf):
            gcol = slice(f * tf, (f + 1) * tf)
            ucol = slice(ff + f * tf, ff + (f + 1) * tf)
            gate = _dot(x, w_ref[0, 0, :, gcol]) + b_ref[0, 0, :, gcol]
            up = _dot(x, w_ref[0, 0, :, ucol]) + b_ref[0, 0, :, ucol]
            gate = jnp.minimum(gate, SWIGLU_LIMIT)
            up = jnp.clip(up, -SWIGLU_LIMIT, SWIGLU_LIMIT)
            act = gate * jax.nn.sigmoid(SWIGLU_ALPHA * gate) * (up + 1.0)
            o_ref[:, gcol] = act.astype(o_ref.dtype)

    @pl.when(cnt_ref[i] == 0)
    def _():
        o_ref[...] = jnp.zeros(o_ref.shape, o_ref.dtype)


def _row_copy(src_ref, src_row, dst_ref, dst_row, sem):
    return pltpu.make_async_copy(src_ref.at[pl.ds(src_row, 1)], dst_ref.at[pl.ds(dst_row, 1)], sem)


def _gather_rows_kernel(idx_ref, src_hbm, o_ref, sem):
    rows = o_ref.shape[0]

    def start(r, carry):
        _row_copy(src_hbm, idx_ref[0, 0, r], o_ref, r, sem).start()
        return carry

    def wait(r, carry):
        _row_copy(src_hbm, 0, o_ref, r, sem).wait()
        return carry

    lax.fori_loop(0, rows, start, 0, unroll=DMA_UNROLL)
    lax.fori_loop(0, rows, wait, 0, unroll=DMA_UNROLL)


def gather_rows(src, idx, blk):
    d = src.shape[1]
    nblk = idx.shape[0] // blk
    return pl.pallas_call(
        _gather_rows_kernel, name="moe_gather", grid=(nblk,),
        in_specs=[pl.BlockSpec((1, 1, blk), lambda i: (i, 0, 0), memory_space=pltpu.SMEM),
                  pl.BlockSpec(memory_space=pl.ANY)],
        out_specs=pl.BlockSpec((blk, d), lambda i: (i, 0)),
        out_shape=jax.ShapeDtypeStruct((nblk * blk, d), src.dtype),
        scratch_shapes=[pltpu.SemaphoreType.DMA(())],
        compiler_params=_cparams(("arbitrary",)))(idx.reshape(nblk, 1, blk), src)


def _moe_down_kernel(be_ref, cnt_ref, tgt_ref, a_ref, w_ref, b_ref, p_ref, y_hbm, ybuf, sem):
    i = pl.program_id(0)
    cnt = cnt_ref[i]

    @pl.when(cnt > 0)
    def _():
        ybuf[...] = (_dot(a_ref[...], w_ref[0, 0]) + b_ref[0, 0]) * p_ref[...]

        def start(r, carry):
            _row_copy(ybuf, r, y_hbm, tgt_ref[0, 0, r], sem).start()
            return carry

        def wait(r, carry):
            _row_copy(ybuf, r, y_hbm, 0, sem).wait()
            return carry

        full = ybuf.shape[0]

        @pl.when(cnt == full)
        def _():
            lax.fori_loop(0, full, start, 0, unroll=DMA_UNROLL)
            lax.fori_loop(0, full, wait, 0, unroll=DMA_UNROLL)

        @pl.when(cnt < full)
        def _():
            lax.fori_loop(0, cnt, start, 0)
            lax.fori_loop(0, cnt, wait, 0)


def moe_experts(xb, row_p, row_tgt, block_e, block_cnt, n_pairs, layer, w_gu, b_gu, w_down, b_down):
    rows, d = xb.shape
    nblk = rows // MOE_BLK
    ff = w_down.shape[2]
    cp = pltpu.CompilerParams(dimension_semantics=("arbitrary",), vmem_limit_bytes=MOE_VMEM_LIMIT)
    act = pl.pallas_call(
        _moe_up_kernel, name="moe_up",
        grid_spec=pltpu.PrefetchScalarGridSpec(
            num_scalar_prefetch=2, grid=(nblk,),
            in_specs=[pl.BlockSpec((MOE_BLK, d), lambda i, be, cnt: (i, 0)),
                      pl.BlockSpec((1, 1, d, 2 * ff), lambda i, be, cnt: (layer, be[i], 0, 0)),
                      pl.BlockSpec((1, 1, 1, 2 * ff), lambda i, be, cnt: (layer, be[i], 0, 0))],
            out_specs=pl.BlockSpec((MOE_BLK, ff), lambda i, be, cnt: (i, 0))),
        out_shape=jax.ShapeDtypeStruct((rows, ff), BF16),
        compiler_params=cp)(block_e, block_cnt, xb, w_gu, b_gu)
    return pl.pallas_call(
        _moe_down_kernel, name="moe_down",
        grid_spec=pltpu.PrefetchScalarGridSpec(
            num_scalar_prefetch=2, grid=(nblk,),
            in_specs=[pl.BlockSpec((1, 1, MOE_BLK), lambda i, be, cnt: (i, 0, 0), memory_space=pltpu.SMEM),
                      pl.BlockSpec((MOE_BLK, ff), lambda i, be, cnt: (i, 0)),
                      pl.BlockSpec((1, 1, ff, d), lambda i, be, cnt: (layer, be[i], 0, 0)),
                      pl.BlockSpec((1, 1, 1, d), lambda i, be, cnt: (layer, be[i], 0, 0)),
                      pl.BlockSpec((MOE_BLK, 1), lambda i, be, cnt: (i, 0))],
            out_specs=pl.BlockSpec(memory_space=pl.ANY),
            scratch_shapes=[pltpu.VMEM((MOE_BLK, d), F32), pltpu.SemaphoreType.DMA(())]),
        out_shape=jax.ShapeDtypeStruct((n_pairs, d), F32),
        compiler_params=cp)(block_e, block_cnt, row_tgt.reshape(nblk, 1, MOE_BLK), act, w_down, b_down, row_p)


def moe_ffn(h, logits, layer, w_gu, b_gu, w_down, b_down):
    n, d = h.shape
    n_pairs = n * TOP_K
    top_val, top_idx = lax.top_k(logits, TOP_K)
    probs = jax.nn.softmax(top_val, axis=-1).reshape(-1)
    flat_e = top_idx.reshape(-1).astype(jnp.int32)
    order = jnp.argsort(flat_e, stable=True).astype(jnp.int32)
    counts = jnp.sum(flat_e[:, None] == jnp.arange(N_EXPERTS, dtype=jnp.int32)[None, :], axis=0, dtype=jnp.int32)
    padded = (counts + MOE_BLK - 1) // MOE_BLK * MOE_BLK
    pad_end = jnp.cumsum(padded)
    pad_start = pad_end - padded
    start = jnp.cumsum(counts) - counts
    nblk = -(-(n_pairs + N_EXPERTS * (MOE_BLK - 1)) // MOE_BLK)
    blk_row0 = jnp.arange(nblk, dtype=jnp.int32) * MOE_BLK
    block_e = jnp.minimum(jnp.sum(blk_row0[:, None] >= pad_end[None, :], axis=1, dtype=jnp.int32), N_EXPERTS - 1)
    blk_off = blk_row0 - pad_start[block_e]
    block_cnt = jnp.clip(counts[block_e] - blk_off, 0, MOE_BLK)
    lane = jnp.arange(MOE_BLK, dtype=jnp.int32)[None, :]
    valid = lane < block_cnt[:, None]
    src = jnp.where(valid, (start[block_e] + blk_off)[:, None] + lane, 0).reshape(-1)
    row_pair = jnp.where(valid.reshape(-1), order[src], 0)
    row_p = jnp.where(valid.reshape(-1), probs[row_pair], 0.0)
    row_tok = row_pair // TOP_K
    xb = gather_rows(h, row_tok, MOE_BLK)
    row_tgt = (row_pair % TOP_K) * n + row_tok
    return moe_experts(xb, row_p.reshape(-1, 1), row_tgt, block_e, block_cnt, n_pairs,
                       layer, w_gu, b_gu, w_down, b_down)


def _split_groups(z):
    return z[:N_PROMPT].reshape(1, SEQ, -1), z[N_PROMPT:].reshape(DEC_BATCH, DEC_SEQ, -1)


def _join_groups(zp, zs):
    return jnp.concatenate([zp.reshape(N_PROMPT, -1), zs.reshape(N_SAMPLE, -1)], axis=0)


def layer(x, mod_g, lp, st, attn_tabs):
    d = D_MODEL
    h = norm_mod(x, lp['norm1_g'], mod_g, 0, 1)
    qkv = matmul(h, lp['w_qkv'], name="proj_qkv", tm=768, tn=512)
    pb = matmul(h, lp['w_pb'], name="proj_rwkv", tm=768, tn=1280)
    bch = matmul(h, lp['w_bch'], name="proj_sconv", tm=768, tn=512)
    pd = matmul(h, lp['w_pd'], name="proj_cconv", tm=768, tn=512)
    gates = matmul(h, lp['w_gl'], name="proj_gates", tm=768, tn=512, act="sigmoid")

    (bias_p, sink_p), (bias_s, sink_s) = attn_tabs
    o_a = jnp.concatenate([
        attn_prompt(qkv, bias_p, sink_p, N_PROMPT),
        attn_sample(qkv, st['k'].reshape(DEC_BATCH, WINDOW, A_KV), st['v'].reshape(DEC_BATCH, WINDOW, A_KV),
                    bias_s, sink_s, N_PROMPT)], axis=0)
    kp, ks = _split_groups(qkv[:, OFF_K:OFF_V])
    vp, vs = _split_groups(qkv[:, OFF_V:OFF_PB])
    new_kv_p = (kp[:, SEQ - WINDOW:].reshape(1, WINDOW, A_KV_HEADS, A_HEAD_DIM),
                vp[:, SEQ - WINDOW:].reshape(1, WINDOW, A_KV_HEADS, A_HEAD_DIM))
    new_kv_s = (ks.reshape(DEC_BATCH, DEC_SEQ, A_KV_HEADS, A_HEAD_DIM),
                vs.reshape(DEC_BATCH, DEC_SEQ, A_KV_HEADS, A_HEAD_DIM))

    pb_p, pb_s = _split_groups(pb)
    ob_p, shift_p, wkv_p = rwkv7_mix(pb_p, jnp.zeros((1, B_PROJ), F32),
                                     jnp.zeros((1, B_HEADS, B_HEAD, B_HEAD), F32), lp)
    ob_s, shift_s, wkv_s = rwkv7_mix(pb_s, st['shift'], st['wkv'], lp)
    o_b = _join_groups(ob_p, ob_s)

    ocp, sconv_p = short_conv(bch, jnp.zeros((1, C_WIDTH - 1, d), F32), lp['sconv_w'],
                              first_row=0, rows=CONV_ROWS)
    ocs, sconv_s = short_conv(bch, st['sconv'], lp['sconv_w'], first_row=N_PROMPT, rows=DEC_SEQ)
    o_c = jnp.concatenate([ocp, ocs], axis=0)
    cargs = (lp['cconv_w'], lp['cconv_b'], lp['cnorm_g'], lp['cnorm_b'])
    odp, cconv_p = conformer_conv(pd, jnp.zeros((1, D_WIDTH - 1, d), F32), *cargs,
                                  first_row=0, rows=CONV_ROWS)
    ods, cconv_s = conformer_conv(pd, st['cconv'], *cargs, first_row=N_PROMPT, rows=DEC_SEQ)
    o_d = jnp.concatenate([odp, ods], axis=0)

    merged = merge_branches([o_a, o_b.astype(BF16), o_c, o_d], lp['w_branch'], lp['layer'], gates)
    x = matmul_resid(merged, lp['w_out'], x, mod_g, 2)

    h2, logits = norm_mod(x, lp['norm2_g'], mod_g, 3, 4, router=(lp['router_w'], lp['router_b']))
    ffn = moe_ffn(h2, logits, lp['layer'], lp['w_gu'], lp['b_gu'], lp['w_down'], lp['b_down'])
    x = resid_gate(x, ffn, mod_g, 5)
    new_p = (new_kv_p[0], new_kv_p[1], wkv_p, shift_p, sconv_p, cconv_p)
    new_s = (new_kv_s[0], new_kv_s[1], wkv_s, shift_s, sconv_s, cconv_s)
    return x, new_p, new_s


def kernel(x_prompt, x_sample, c_prompt, c_sample, cache_k, cache_v, state_wkv, state_shift, state_sconv, state_cconv, w_mod, b_mod, norm1_g, norm2_g, w_in, attn_sinks, rel_bias_table, rwkv_mu, rwkv_w0, rwkv_w2, rwkv_a0, rwkv_a2, rwkv_g2, rwkv_k_k, rwkv_k_a, rwkv_r_k, rwkv_ln_g, rwkv_ln_b, sconv_w, cconv_w, cconv_b, cnorm_g, cnorm_b, w_branch, w_out, router_w, router_b, expert_w_gu, expert_b_gu, expert_w_down, expert_b_down, final_g):
    d = D_MODEL
    x = jnp.concatenate([x_prompt.reshape(N_PROMPT, d), x_sample.reshape(N_SAMPLE, d)], axis=0)
    n_seq = BATCH + DEC_BATCH
    c_act = jnp.pad(jnp.concatenate([jax.nn.silu(c_prompt), jax.nn.silu(c_sample)], axis=0),
                    ((0, 16 - n_seq), (0, 0)))

    blk_pos = jnp.arange(CHUNK, dtype=jnp.int32)
    qpos_p = WINDOW + blk_pos
    kpos_p = jnp.arange(WINDOW + CHUNK, dtype=jnp.int32)
    qpos_s = PAST_LEN + jnp.arange(DEC_SEQ, dtype=jnp.int32)
    kpos_s = PAST_LEN - WINDOW + jnp.arange(WINDOW + DEC_SEQ, dtype=jnp.int32)

    w_branch_bf = w_branch.astype(BF16)
    w_gu_bf = expert_w_gu.astype(BF16)
    w_down_bf = expert_w_down.astype(BF16)
    b_gu4 = expert_b_gu.reshape(DEPTH, N_EXPERTS, 1, -1)
    b_down4 = expert_b_down.reshape(DEPTH, N_EXPERTS, 1, -1)

    st_p, st_s = [], []
    for l in range(DEPTH):
        w_in_l = w_in[l]
        lp = dict(norm1_g=norm1_g[l], norm2_g=norm2_g[l],
                  w_qkv=w_in_l[:, OFF_Q:OFF_PB].astype(BF16), w_pb=w_in_l[:, OFF_PB:OFF_BG].astype(BF16),
                  w_bch=w_in_l[:, OFF_BG:OFF_PD].astype(BF16), w_pd=w_in_l[:, OFF_PD:OFF_GL].astype(BF16),
                  w_gl=w_in_l[:, OFF_GL:].astype(BF16),
                  mu=rwkv_mu[l], w0=rwkv_w0[l], w2=rwkv_w2[l], a0=rwkv_a0[l], a2=rwkv_a2[l],
                  g2=rwkv_g2[l], k_k=rwkv_k_k[l], k_a=rwkv_k_a[l], r_k=rwkv_r_k[l], ln_g=rwkv_ln_g[l],
                  ln_b=rwkv_ln_b[l], sconv_w=sconv_w[l], cconv_w=cconv_w[l], cconv_b=cconv_b[l],
                  cnorm_g=cnorm_g[l], cnorm_b=cnorm_b[l], layer=l, w_branch=w_branch_bf,
                  w_out=w_out[l].astype(BF16), router_w=router_w[l], router_b=router_b[l],
                  w_gu=w_gu_bf, b_gu=b_gu4, w_down=w_down_bf, b_down=b_down4)
        mod = matmul(c_act, w_mod[l].astype(BF16), name="modulation", tm=16, tn=512)[:n_seq] + b_mod[l]
        mod_g = jnp.concatenate([jnp.broadcast_to(mod[:1], (N_PROMPT // MOD_GROUP, 6 * d)),
                                 jnp.repeat(mod[1:], DEC_SEQ // MOD_GROUP, axis=0)], axis=0)
        tabs = (_attn_tables(qpos_p, kpos_p, rel_bias_table, attn_sinks[l], False),
                _attn_tables(qpos_s, kpos_s, rel_bias_table, attn_sinks[l], True))
        st = dict(k=cache_k[l], v=cache_v[l], shift=state_shift[l], wkv=state_wkv[l],
                  sconv=state_sconv[l], cconv=state_cconv[l])
        x, sp, ss = layer(x, mod_g, lp, st, tabs)
        st_p.append(sp)
        st_s.append(ss)
    y = final_norm(x, final_g)
    y_prompt = y[:N_PROMPT].reshape(1, SEQ, d)
    y_sample = y[N_PROMPT:].reshape(DEC_BATCH, DEC_SEQ, d)
    return (y_prompt, y_sample,
            jnp.stack([s[0] for s in st_p]), jnp.stack([s[1] for s in st_p]), jnp.stack([s[2] for s in st_p]),
            jnp.stack([s[3] for s in st_p]), jnp.stack([s[4] for s in st_p]), jnp.stack([s[5] for s in st_p]),
            jnp.stack([s[0] for s in st_s]), jnp.stack([s[1] for s in st_s]), jnp.stack([s[2] for s in st_s]),
            jnp.stack([s[3] for s in st_s]), jnp.stack([s[4] for s in st_s]), jnp.stack([s[5] for s in st_s]))
```

```python
import functools
import math

import jax
import jax.numpy as jnp
from jax import lax
from jax.experimental import pallas as pl
from jax.experimental.pallas import tpu as pltpu

F32 = jnp.float32
BF16 = jnp.bfloat16

D_MODEL = 2048
BATCH = 1
SEQ = 8192
DEPTH = 2
DEC_BATCH = 8
DEC_SEQ = 32
PAST_LEN = 1024
CHUNK = 64
NORM_EPS = 1e-5
NEG_INF = -1e30

A_HEAD_DIM = 64
A_HEADS = 32
A_KV_HEADS = 8
A_GROUP = 4
WINDOW = 128
WIN_CHUNKS = 2
NUM_BUCKETS = 32
MAX_DISTANCE = 128
A_Q = A_HEADS * A_HEAD_DIM
A_KV = A_KV_HEADS * A_HEAD_DIM

B_HEAD = 64
B_HEADS = 32
B_DIM = 2048
B_DECAY_RANK = 64
B_ICLR_RANK = 64
B_GATE_RANK = 128
B_GN_EPS = 64e-5
B_PROJ = 3 * B_DIM + B_DECAY_RANK + B_ICLR_RANK + B_GATE_RANK

C_WIDTH = 3
D_WIDTH = 31
N_BRANCH = 4

N_EXPERTS = 32
TOP_K = 4
D_FF = D_MODEL
SWIGLU_LIMIT = 7.0
SWIGLU_ALPHA = 1.702

OFF_Q = 0
OFF_K = OFF_Q + A_Q
OFF_V = OFF_K + A_KV
OFF_PB = OFF_V + A_KV
OFF_BG = OFF_PB + B_PROJ
OFF_CG = OFF_BG + D_MODEL
OFF_HC = OFF_CG + D_MODEL
OFF_PD = OFF_HC + D_MODEL
OFF_GL = OFF_PD + 2 * D_MODEL
IN_TOTAL = OFF_GL + N_BRANCH * D_MODEL

N_PROMPT = SEQ
N_SAMPLE = DEC_BATCH * DEC_SEQ
N_TOK = N_PROMPT + N_SAMPLE
MOD_GROUP = 32
N_MOD_GROUPS = N_TOK // MOD_GROUP

LANES = 128
RWKV_T = 64
RWKV_PAIRS = 16
RWKV_SEQ_T = 16
RWKV_TAIL = 256
PROJ_TM = 1408
CONV_ROWS = 256
CONV_STRIP = 512
CONV_C_HALO = 8
CONV_D_HALO = 32
MOE_BLK = 256
MOE_FF_TILE = 512
DMA_UNROLL = 8
VMEM_LIMIT = 48 * 1024 * 1024
MOE_VMEM_LIMIT = 56 * 1024 * 1024


def _cparams(sem):
    return pltpu.CompilerParams(dimension_semantics=sem, vmem_limit_bytes=VMEM_LIMIT)


def _dot(a, b):
    return jnp.dot(a, b, preferred_element_type=F32)


def _dot_nt(a, b):
    return lax.dot_general(a, b, (((1,), (1,)), ((), ())), preferred_element_type=F32)


def _mm_kernel(x_ref, w_ref, o_ref, *, act):
    y = _dot(x_ref[...].astype(BF16), w_ref[...])
    if act == "sigmoid":
        y = jax.nn.sigmoid(y)
    o_ref[...] = y.astype(o_ref.dtype)


def matmul(x, w, *, name, tm, tn, out_dtype=F32, act=None):
    m, k = x.shape
    n = w.shape[1]
    assert m % tm == 0 and n % tn == 0
    return pl.pallas_call(
        functools.partial(_mm_kernel, act=act), name=name,
        grid=(m // tm, n // tn),
        in_specs=[pl.BlockSpec((tm, k), lambda i, j: (i, 0)),
                  pl.BlockSpec((k, tn), lambda i, j: (0, j))],
        out_specs=pl.BlockSpec((tm, tn), lambda i, j: (i, j)),
        out_shape=jax.ShapeDtypeStruct((m, n), out_dtype),
        compiler_params=_cparams(("parallel", "parallel")),
    )(x, w)


def _group_rows(gi):
    return slice(gi * MOD_GROUP, (gi + 1) * MOD_GROUP)


def _norm_mod_rows(x_ref, g_ref, sh_ref, sc_ref, gi):
    x = x_ref[_group_rows(gi), :]
    y = x * lax.rsqrt(jnp.mean(x * x, axis=-1, keepdims=True) + NORM_EPS) * g_ref[...]
    return y * (1.0 + sc_ref[gi:gi + 1, :]) + sh_ref[gi:gi + 1, :]


def _norm_mod_kernel(x_ref, g_ref, sh_ref, sc_ref, h_ref):
    for gi in range(x_ref.shape[0] // MOD_GROUP):
        h_ref[_group_rows(gi), :] = _norm_mod_rows(x_ref, g_ref, sh_ref, sc_ref, gi).astype(h_ref.dtype)


def _norm_mod_router_kernel(x_ref, g_ref, sh_ref, sc_ref, rw_ref, rb_ref, h_ref, lg_ref):
    w = rw_ref[...].astype(BF16)
    for gi in range(x_ref.shape[0] // MOD_GROUP):
        h = _norm_mod_rows(x_ref, g_ref, sh_ref, sc_ref, gi).astype(BF16)
        h_ref[_group_rows(gi), :] = h.astype(h_ref.dtype)
        lg_ref[_group_rows(gi), :] = _dot(h, w) + rb_ref[...]


def norm_mod(x, g, mod_g, shift_idx, scale_idx, router=None, tm=256):
    m, d = x.shape
    gm = tm // MOD_GROUP
    in_specs = [pl.BlockSpec((tm, d), lambda i: (i, 0)),
                pl.BlockSpec((1, d), lambda i: (0, 0)),
                pl.BlockSpec((gm, d), lambda i: (i, shift_idx)),
                pl.BlockSpec((gm, d), lambda i: (i, scale_idx))]
    args = [x, g.reshape(1, d), mod_g, mod_g]
    if router is None:
        return pl.pallas_call(
            _norm_mod_kernel, name="norm_mod", grid=(m // tm,), in_specs=in_specs,
            out_specs=pl.BlockSpec((tm, d), lambda i: (i, 0)),
            out_shape=jax.ShapeDtypeStruct((m, d), BF16),
            compiler_params=_cparams(("parallel",)))(*args)
    rw, rb = router
    ne = rw.shape[1]
    in_specs += [pl.BlockSpec((d, ne), lambda i: (0, 0)), pl.BlockSpec((1, ne), lambda i: (0, 0))]
    args += [rw, rb.reshape(1, ne)]
    return pl.pallas_call(
        _norm_mod_router_kernel, name="norm_router", grid=(m // tm,), in_specs=in_specs,
        out_specs=[pl.BlockSpec((tm, d), lambda i: (i, 0)), pl.BlockSpec((tm, ne), lambda i: (i, 0))],
        out_shape=[jax.ShapeDtypeStruct((m, d), F32), jax.ShapeDtypeStruct((m, ne), F32)],
        compiler_params=_cparams(("parallel",)))(*args)


def _final_norm_kernel(x_ref, g_ref, o_ref):
    x = x_ref[...]
    o_ref[...] = x * lax.rsqrt(jnp.mean(x * x, axis=-1, keepdims=True) + NORM_EPS) * g_ref[...]


def final_norm(x, g, tm=256):
    m, d = x.shape
    return pl.pallas_call(
        _final_norm_kernel, name="final_norm", grid=(m // tm,),
        in_specs=[pl.BlockSpec((tm, d), lambda i: (i, 0)), pl.BlockSpec((1, d), lambda i: (0, 0))],
        out_specs=pl.BlockSpec((tm, d), lambda i: (i, 0)),
        out_shape=jax.ShapeDtypeStruct((m, d), F32),
        compiler_params=_cparams(("parallel",)))(x, g.reshape(1, d))


def _mm_resid_kernel(a_ref, w_ref, x_ref, gate_ref, o_ref):
    y = _dot(a_ref[...], w_ref[...])
    for gi in range(y.shape[0] // MOD_GROUP):
        rows = _group_rows(gi)
        o_ref[rows, :] = x_ref[rows, :] + gate_ref[gi:gi + 1, :] * y[rows]


def matmul_resid(a, w, x, mod_g, gate_idx, *, tm=768, tn=512):
    m, k = a.shape
    n = w.shape[1]
    gm = tm // MOD_GROUP
    nj = n // tn
    return pl.pallas_call(
        _mm_resid_kernel, name="out_proj", grid=(m // tm, nj),
        in_specs=[pl.BlockSpec((tm, k), lambda i, j: (i, 0)),
                  pl.BlockSpec((k, tn), lambda i, j: (0, j)),
                  pl.BlockSpec((tm, tn), lambda i, j: (i, j)),
                  pl.BlockSpec((gm, tn), lambda i, j: (i, gate_idx * nj + j))],
        out_specs=pl.BlockSpec((tm, tn), lambda i, j: (i, j)),
        out_shape=jax.ShapeDtypeStruct((m, n), F32),
        compiler_params=_cparams(("parallel", "parallel")))(a, w, x, mod_g)


def _resid_kernel(x_ref, gate_ref, *refs):
    f_refs, o_ref = refs[:TOP_K], refs[TOP_K]
    for gi in range(x_ref.shape[0] // MOD_GROUP):
        rows = _group_rows(gi)
        ffn = f_refs[0][rows, :]
        for k in range(1, TOP_K):
            ffn = ffn + f_refs[k][rows, :]
        o_ref[rows, :] = x_ref[rows, :] + gate_ref[gi:gi + 1, :] * ffn


def resid_gate(x, f_pairs, mod_g, gate_idx, tm=256):
    m, d = x.shape
    gm = tm // MOD_GROUP
    nt = m // tm
    fspec = lambda k: pl.BlockSpec((tm, d), lambda i: (k * nt + i, 0))
    return pl.pallas_call(
        _resid_kernel, name="ffn_resid", grid=(nt,),
        in_specs=[pl.BlockSpec((tm, d), lambda i: (i, 0)), pl.BlockSpec((gm, d), lambda i: (i, gate_idx))]
                 + [fspec(k) for k in range(TOP_K)],
        out_specs=pl.BlockSpec((tm, d), lambda i: (i, 0)),
        out_shape=jax.ShapeDtypeStruct((m, d), F32),
        compiler_params=_cparams(("parallel",)))(x, mod_g, *([f_pairs] * TOP_K))


def _merge_kernel(ba_ref, bb_ref, bc_ref, bd_ref, w_ref, ga_ref, gb_ref, gc_ref, gd_ref, o_ref):
    acc = None
    for n, (b_ref, g_ref) in enumerate(((ba_ref, ga_ref), (bb_ref, gb_ref), (bc_ref, gc_ref), (bd_ref, gd_ref))):
        part = g_ref[...] * _dot(b_ref[...], w_ref[0, n])
        acc = part if acc is None else acc + part
    o_ref[...] = acc.astype(o_ref.dtype)


def merge_branches(branches, w_branch, layer, gates, *, tm=768, tn=256):
    m, k = branches[0].shape
    d = w_branch.shape[3]
    nj = d // tn
    bspec = pl.BlockSpec((tm, k), lambda i, j: (i, 0))
    gspec = lambda n: pl.BlockSpec((tm, tn), lambda i, j: (i, n * nj + j))
    return pl.pallas_call(
        _merge_kernel, name="merge", grid=(m // tm, nj),
        in_specs=[bspec] * N_BRANCH + [pl.BlockSpec((1, N_BRANCH, k, tn), lambda i, j: (layer, 0, 0, j))]
                 + [gspec(n) for n in range(N_BRANCH)],
        out_specs=pl.BlockSpec((tm, tn), lambda i, j: (i, j)),
        out_shape=jax.ShapeDtypeStruct((m, d), BF16),
        compiler_params=_cparams(("parallel", "parallel")))(*branches, w_branch, gates, gates, gates, gates)


def _attn_core(q, k, v, bias_ref, sink_ref, o_ref, key_ok):
    sq = q.shape[0]
    lane = lax.broadcasted_iota(jnp.int32, (1, LANES), 1)
    low = lane < A_HEAD_DIM
    q = q * (A_HEAD_DIM ** -0.5)
    groups = range(A_KV_HEADS)
    k2, v2 = [], []
    for kvp in range(A_KV_HEADS // 2):
        kp = k[:, kvp * LANES:(kvp + 1) * LANES]
        vp = v[:, kvp * LANES:(kvp + 1) * LANES]
        kp_sw = pltpu.roll(kp, A_HEAD_DIM, 1)
        vp_sw = pltpu.roll(vp, A_HEAD_DIM, 1)
        k2 += [jnp.where(low, kp, kp_sw).astype(BF16), jnp.where(low, kp_sw, kp).astype(BF16)]
        v2 += [jnp.where(low, vp, vp_sw).astype(BF16), jnp.where(low, vp_sw, vp).astype(BF16)]
    lhs = []
    for g in groups:
        parts = []
        for j in range(2):
            qp = q[:, (2 * g + j) * LANES:(2 * g + j + 1) * LANES]
            parts.append(jnp.where(low, qp, 0.0))
            parts.append(jnp.where(low, 0.0, qp))
        lhs.append(jnp.concatenate(parts, axis=0).astype(BF16))
    s = [_dot_nt(lhs[g], k2[g]) + bias_ref[g] for g in groups]
    if key_ok is not None:
        s = [jnp.where(key_ok, z, NEG_INF) for z in s]
    sink = [sink_ref[g] for g in groups]
    m = [jnp.maximum(jnp.max(s[g], axis=-1, keepdims=True), sink[g]) for g in groups]
    p = [jnp.exp(s[g] - m[g]) for g in groups]
    p = [p[g] / (jnp.sum(p[g], axis=-1, keepdims=True) + jnp.exp(sink[g] - m[g])) for g in groups]
    o2 = [_dot(p[g].astype(BF16), v2[g]) for g in groups]
    for g in groups:
        for j in range(2):
            oj = jnp.where(low, o2[g][(2 * j) * sq:(2 * j + 1) * sq], o2[g][(2 * j + 1) * sq:(2 * j + 2) * sq])
            o_ref[:, (2 * g + j) * LANES:(2 * g + j + 1) * LANES] = oj.astype(o_ref.dtype)


def _attn_prompt_kernel(q_ref, k0_ref, k1_ref, k2_ref, v0_ref, v1_ref, v2_ref, bias_ref, sink_ref, o_ref):
    c = pl.program_id(0)
    k = jnp.concatenate([k0_ref[...], k1_ref[...], k2_ref[...]], axis=0)
    v = jnp.concatenate([v0_ref[...], v1_ref[...], v2_ref[...]], axis=0)
    sk = k.shape[0]
    k_pos = (c - WIN_CHUNKS) * CHUNK + lax.broadcasted_iota(jnp.int32, (1, sk), 1)
    _attn_core(q_ref[...], k, v, bias_ref, sink_ref, o_ref, k_pos >= 0)


_QKV_KCOL = A_Q // A_KV
_QKV_VCOL = _QKV_KCOL + 1


def attn_prompt(qkv, bias, sink_col, n_prompt):
    nc = n_prompt // CHUNK
    kspec = lambda d, col: pl.BlockSpec((CHUNK, A_KV), lambda c: (jnp.maximum(c - d, 0), col))
    return pl.pallas_call(
        _attn_prompt_kernel, name="attn_prompt", grid=(nc,),
        in_specs=[pl.BlockSpec((CHUNK, A_Q), lambda c: (c, 0)),
                  kspec(2, _QKV_KCOL), kspec(1, _QKV_KCOL), kspec(0, _QKV_KCOL),
                  kspec(2, _QKV_VCOL), kspec(1, _QKV_VCOL), kspec(0, _QKV_VCOL),
                  pl.BlockSpec(bias.shape, lambda c: (0, 0, 0)),
                  pl.BlockSpec(sink_col.shape, lambda c: (0, 0, 0))],
        out_specs=pl.BlockSpec((CHUNK, A_Q), lambda c: (c, 0)),
        out_shape=jax.ShapeDtypeStruct((n_prompt, A_Q), BF16),
        compiler_params=_cparams(("parallel",)))(qkv, qkv, qkv, qkv, qkv, qkv, qkv, bias, sink_col)


def _attn_sample_kernel(q_ref, kn_ref, vn_ref, kc_ref, vc_ref, bias_ref, sink_ref, o_ref):
    k = jnp.concatenate([kc_ref[0], kn_ref[...]], axis=0)
    v = jnp.concatenate([vc_ref[0], vn_ref[...]], axis=0)
    _attn_core(q_ref[...], k, v, bias_ref, sink_ref, o_ref, None)


def attn_sample(qkv, k_cache, v_cache, bias, sink_col, n_prompt):
    nb = k_cache.shape[0]
    base = n_prompt // DEC_SEQ
    return pl.pallas_call(
        _attn_sample_kernel, name="attn_sample", grid=(nb,),
        in_specs=[pl.BlockSpec((DEC_SEQ, A_Q), lambda b: (base + b, 0)),
                  pl.BlockSpec((DEC_SEQ, A_KV), lambda b: (base + b, _QKV_KCOL)),
                  pl.BlockSpec((DEC_SEQ, A_KV), lambda b: (base + b, _QKV_VCOL)),
                  pl.BlockSpec((1, WINDOW, A_KV), lambda b: (b, 0, 0)),
                  pl.BlockSpec((1, WINDOW, A_KV), lambda b: (b, 0, 0)),
                  pl.BlockSpec(bias.shape, lambda b: (0, 0, 0)),
                  pl.BlockSpec(sink_col.shape, lambda b: (0, 0, 0))],
        out_specs=pl.BlockSpec((DEC_SEQ, A_Q), lambda b: (b, 0)),
        out_shape=jax.ShapeDtypeStruct((nb * DEC_SEQ, A_Q), BF16),
        compiler_params=_cparams(("parallel",)))(qkv, qkv, qkv, k_cache, v_cache, bias, sink_col)


def _t5_bucket(rel):
    nb = NUM_BUCKETS // 2
    max_exact = nb // 2
    ret = jnp.where(rel > 0, nb, 0)
    n = jnp.abs(rel)
    nf = jnp.maximum(n, 1).astype(F32)
    large = max_exact + (jnp.log(nf / max_exact) / math.log(MAX_DISTANCE / max_exact) * (nb - max_exact)).astype(jnp.int32)
    large = jnp.minimum(large, nb - 1)
    return ret + jnp.where(n < max_exact, n, large)


def _attn_tables(q_pos, k_pos, table, sinks, mask_positions):
    sq, sk = q_pos.shape[0], k_pos.shape[0]
    b = table[_t5_bucket(k_pos[None, :] - q_pos[:, None])]
    b = jnp.transpose(b, (2, 0, 1)).astype(F32)
    if mask_positions:
        qc, kc = q_pos[:, None] // CHUNK, k_pos[None, :] // CHUNK
        vis = (k_pos[None, :] >= 0) & (kc <= qc) & (kc >= qc - WIN_CHUNKS)
        b = jnp.where(vis[None], b, NEG_INF)
    bias = b.reshape(A_KV_HEADS, A_GROUP * sq, sk)
    sink_col = jnp.broadcast_to(sinks.astype(F32).reshape(A_KV_HEADS, A_GROUP, 1, 1),
                                (A_KV_HEADS, A_GROUP, sq, 1)).reshape(A_KV_HEADS, A_GROUP * sq, 1)
    return bias, sink_col


def _rwkv_kernel(r_ref, lw_ref, k_ref, v_ref, kk_ref, a_ref, s0_ref, y_ref, st_ref, s_scr):
    c = pl.program_id(2)
    t = RWKV_T

    @pl.when(c == 0)
    def _():
        s_scr[...] = s0_ref[0]

    row = lax.broadcasted_iota(jnp.int32, (t, LANES), 0)
    lane = lax.broadcasted_iota(jnp.int32, (t, LANES), 1)
    low = lane < B_HEAD
    col = lane & (t - 1)
    strict = col < row
    incl = col <= row
    tri = jnp.where(lax.broadcasted_iota(jnp.int32, (t, t), 1) <= lax.broadcasted_iota(jnp.int32, (t, t), 0),
                    1.0, 0.0).astype(BF16)
    r128 = lax.broadcasted_iota(jnp.int32, (LANES, LANES), 0)
    c128 = lax.broadcasted_iota(jnp.int32, (LANES, LANES), 1)
    same_head = (r128 < B_HEAD) == (c128 < B_HEAD)

    def stack(x):
        return jnp.concatenate([jnp.where(low, x, 0.0), jnp.where(low, 0.0, x)], axis=0)

    pairs = range(RWKV_PAIRS)
    sls = [slice(p * LANES, (p + 1) * LANES) for p in pairs]
    n_steps = int(math.log2(t))

    def prepare(sl):
        lw = lw_ref[0, :, sl]
        hi = lw.astype(BF16)
        rem = lw - hi.astype(F32)
        mid = rem.astype(BF16)
        lo = (rem - mid.astype(F32)).astype(BF16)
        cs = _dot(tri, hi) + _dot(tri, mid) + _dot(tri, lo)
        cs_end = cs[t - 1:t, :]
        kk = kk_ref[0, :, sl]
        ka = kk * a_ref[0, :, sl]
        kx = k_ref[0, :, sl]
        g_inv = jnp.exp(-cs)
        g_end = jnp.exp(cs_end - cs)
        a_t = -kk * jnp.exp(cs - lw)
        r_t = r_ref[0, :, sl] * jnp.exp(cs)
        lhs = jnp.concatenate([a_t, r_t], axis=0).astype(BF16)
        rhs = jnp.concatenate([stack(ka * g_inv), stack(kx * g_inv)], axis=0).astype(BF16)
        tail = jnp.concatenate([ka * g_end, kx * g_end], axis=0).astype(BF16)
        return lhs, rhs, tail, jnp.exp(cs_end)

    prep = [prepare(sl) for sl in sls]
    vs = [v_ref[0, :, sl] for sl in sls]
    st_v = [stack(v).astype(BF16) for v in vs]
    ss = [s_scr[p] for p in pairs]
    d1 = [_dot_nt(prep[p][0], prep[p][1]) for p in pairs]
    d2 = [_dot_nt(prep[p][0], ss[p].astype(BF16)) for p in pairs]
    l_ak = [jnp.where(strict, d1[p][:t, 2 * t:], 0.0).astype(BF16) for p in pairs]
    pw = [jnp.where(strict, d1[p][:t, :2 * t], 0.0) for p in pairs]
    l_r = [jnp.concatenate([jnp.where(incl, d1[p][t:, :2 * t], 0.0), jnp.where(incl, d1[p][t:, 2 * t:], 0.0)],
                           axis=1).astype(BF16) for p in pairs]
    u = [d2[p][:t] + _dot(l_ak[p], st_v[p]) for p in pairs]
    for i in range(n_steps):
        u = [u[p] + _dot(pw[p].astype(BF16), stack(u[p]).astype(BF16)) for p in pairs]
        if i < n_steps - 1:
            pw = [_dot(pw[p].astype(BF16), stack(pw[p]).astype(BF16)) for p in pairs]
    y = [d2[p][t:] + _dot(l_r[p], jnp.concatenate([stack(u[p]).astype(BF16), st_v[p]], axis=0)) for p in pairs]
    upd = [_dot(jnp.concatenate([u[p], vs[p]], axis=0).T.astype(BF16), prep[p][2]) for p in pairs]
    for p in pairs:
        s_scr[p] = ss[p] * prep[p][3] + jnp.where(same_head, upd[p], 0.0)
        y_ref[0, :, sls[p]] = y[p]

    @pl.when(c == pl.num_programs(2) - 1)
    def _():
        st_ref[0] = s_scr[...]


def rwkv_scan(r, lw, k, v, kk, a, s0, l):
    b, _, d = r.shape
    npair = d // LANES
    ng = npair // RWKV_PAIRS
    w = RWKV_PAIRS * LANES
    seq = pl.BlockSpec((1, RWKV_T, w), lambda bi, gi, ci: (bi, ci, gi))
    sspec = pl.BlockSpec((1, RWKV_PAIRS, LANES, LANES), lambda bi, gi, ci: (bi, gi, 0, 0))
    return pl.pallas_call(
        _rwkv_kernel, name="rwkv_chunk", grid=(b, ng, l // RWKV_T),
        in_specs=[seq] * 6 + [sspec],
        out_specs=[seq, sspec],
        out_shape=[jax.ShapeDtypeStruct((b, l, d), F32), jax.ShapeDtypeStruct(s0.shape, F32)],
        scratch_shapes=[pltpu.VMEM((RWKV_PAIRS, LANES, LANES), F32)],
        compiler_params=_cparams(("parallel", "parallel", "arbitrary")))(r, lw, k, v, kk, a, s0)


def _rwkv_seq_kernel(r_ref, w_ref, k_ref, v_ref, kk_ref, a_ref, s0_ref, y_ref, st_ref, s_scr, sb_scr, vk_scr):
    c = pl.program_id(1)
    t_len = r_ref.shape[1]
    npair = r_ref.shape[2]
    n = B_HEAD

    @pl.when(c == 0)
    def _():
        s_scr[...] = s0_ref[0]
        sb_scr[...] = s0_ref[0].astype(BF16)

    low = lax.broadcasted_iota(jnp.int32, (1, LANES), 1) < n
    eye = jnp.where((lax.broadcasted_iota(jnp.int32, (n, LANES), 1) & (n - 1))
                    == lax.broadcasted_iota(jnp.int32, (n, LANES), 0), 1.0, 0.0).astype(BF16)
    eye3 = jnp.concatenate([eye, eye, eye], axis=1)

    def per_head_rows(x, rows):
        x0 = jnp.where(low, x, 0.0).astype(BF16)
        x1 = jnp.where(low, 0.0, x).astype(BF16)
        return jnp.concatenate([jnp.broadcast_to(x0, (rows, LANES)), jnp.broadcast_to(x1, (rows, LANES))], axis=0)

    def outer(t, carry):
        v_all, k_all = v_ref[0, t], k_ref[0, t]
        hi = v_all.astype(BF16).astype(F32)
        mid = (v_all - hi).astype(BF16).astype(F32)
        lo = v_all - hi - mid
        wmats = [jnp.concatenate([per_head_rows(z[p:p + 1], n) for z in (hi, mid, lo)], axis=1) for p in range(npair)]
        vcols = [_dot_nt(eye3, wmat) for wmat in wmats]
        for p in range(npair):
            vk_scr[p, t] = vcols[p] * k_all[p:p + 1]
        return carry

    lax.fori_loop(0, t_len, outer, 0)

    def step(t, carry):
        kk_all, a_all, w_all, r_all = kk_ref[0, t], a_ref[0, t], w_ref[0, t], r_ref[0, t]
        ka_all = kk_all * a_all
        pairs = range(npair)
        m_kk = [per_head_rows(-kk_all[p:p + 1], n) for p in pairs]
        m_r = [per_head_rows(r_all[p:p + 1], 8) for p in pairs]
        sa = [_dot_nt(sb_scr[p], m_kk[p]) for p in pairs]
        s = [s_scr[p] * w_all[p:p + 1] + sa[p] * ka_all[p:p + 1] + vk_scr[p, t] for p in pairs]
        sb = [z.astype(BF16) for z in s]
        for p in pairs:
            s_scr[p] = s[p]
            sb_scr[p] = sb[p]
        yy = [_dot_nt(m_r[p], jnp.concatenate([sb[p], sb[p]], axis=0)) for p in pairs]
        for p in pairs:
            y_ref[0, t, p:p + 1, :] = jnp.where(low, yy[p][0:1], yy[p][8:9])
        return carry

    lax.fori_loop(0, t_len, step, 0)

    @pl.when(c == pl.num_programs(1) - 1)
    def _():
        st_ref[0] = s_scr[...]


def rwkv_seq(r, w, k, v, kk, a, s0, t_blk=RWKV_SEQ_T):
    b, l, d = r.shape
    npair = d // LANES
    t_blk = min(t_blk, l)
    assert l % t_blk == 0
    seq = pl.BlockSpec((1, t_blk, npair, LANES), lambda bi, ci: (bi, ci, 0, 0))
    sspec = pl.BlockSpec((1, npair, B_HEAD, LANES), lambda bi, ci: (bi, 0, 0, 0))
    y, s_new = pl.pallas_call(
        _rwkv_seq_kernel, name="rwkv_seq", grid=(b, l // t_blk),
        in_specs=[seq] * 6 + [sspec],
        out_specs=[seq, sspec],
        out_shape=[jax.ShapeDtypeStruct((b, l, npair, LANES), F32), jax.ShapeDtypeStruct(s0.shape, F32)],
        scratch_shapes=[pltpu.VMEM((npair, B_HEAD, LANES), F32), pltpu.VMEM((npair, B_HEAD, LANES), BF16),
                        pltpu.VMEM((npair, t_blk, B_HEAD, LANES), F32)],
        compiler_params=_cparams(("arbitrary", "arbitrary")))(
            *[z.reshape(b, l, npair, LANES) for z in (r, w, k, v, kk, a)], s0)
    return y.reshape(b, l, d), s_new


def _pair_states(s):
    b = s.shape[0]
    s = s.reshape(b, B_HEADS // 2, 2, B_HEAD, B_HEAD)
    return jnp.swapaxes(s, 2, 3).reshape(b, B_HEADS // 2, B_HEAD, 2 * B_HEAD)


def _unpair_states(sp):
    b = sp.shape[0]
    s = sp.reshape(b, B_HEADS // 2, B_HEAD, 2, B_HEAD)
    return jnp.swapaxes(s, 2, 3).reshape(b, B_HEADS, B_HEAD, B_HEAD)


def _pack_states(s):
    b = s.shape[0]
    s = s.reshape(b, B_HEADS // 2, 2, B_HEAD, B_HEAD)
    z = jnp.zeros_like(s[:, :, 0])
    top = jnp.concatenate([s[:, :, 0], z], axis=-1)
    bot = jnp.concatenate([z, s[:, :, 1]], axis=-1)
    return jnp.concatenate([top, bot], axis=-2)


def _unpack_states(sp):
    b = sp.shape[0]
    s0 = sp[:, :, :B_HEAD, :B_HEAD]
    s1 = sp[:, :, B_HEAD:, B_HEAD:]
    return jnp.stack([s0, s1], axis=2).reshape(b, B_HEADS, B_HEAD, B_HEAD)


def rwkv7_mix(pb, shift_prev, wkv_prev, lp):
    b, t, _ = pb.shape
    prev = jnp.concatenate([shift_prev[:, None], pb[:, :-1]], axis=1)
    xm = pb + (prev - pb) * lp['mu']
    splits = [B_DIM, 2 * B_DIM, 3 * B_DIM, 3 * B_DIM + B_DECAY_RANK, 3 * B_DIM + B_DECAY_RANK + B_ICLR_RANK]
    r, k, v, wl, al, gl = jnp.split(xm, splits, axis=-1)
    logw = -jax.nn.softplus(-(lp['w0'] + jnp.tanh(wl) @ lp['w2'])) - 0.5
    lw = -jnp.exp(logw)
    a = jax.nn.sigmoid(lp['a0'] + al @ lp['a2'])
    g = jax.nn.sigmoid(gl) @ lp['g2']
    heads = lambda z: z.reshape(b, t, B_HEADS, B_HEAD)
    kkh = heads(k * lp['k_k'])
    kkh = kkh * lax.rsqrt(jnp.sum(kkh * kkh, axis=-1, keepdims=True) + 1e-12)
    kk = kkh.reshape(b, t, B_DIM)
    k2 = k * (1.0 + (a - 1.0) * lp['k_a'])
    t_head = max(t - RWKV_TAIL, 0) // RWKV_T * RWKV_T
    seqs = [r, lw, k2, v, kk, a]
    state = wkv_prev
    ys = []
    if t_head:
        y_head, s_bd = rwkv_scan(*seqs, _pack_states(state), t_head)
        state = _unpack_states(s_bd)
        ys.append(y_head)
    seqs[1] = jnp.exp(lw)
    y_tail, s_pair = rwkv_seq(*[z[:, t_head:] for z in seqs], _pair_states(state))
    ys.append(y_tail)
    y = heads(jnp.concatenate(ys, axis=1))
    mu = jnp.mean(y, axis=-1, keepdims=True)
    var = jnp.mean(jnp.square(y - mu), axis=-1, keepdims=True)
    yn = ((y - mu) * lax.rsqrt(var + B_GN_EPS)).reshape(b, t, B_DIM) * lp['ln_g'] + lp['ln_b']
    bonus = jnp.sum(heads(r) * heads(k2) * lp['r_k'], axis=-1, keepdims=True) * heads(v)
    o = (yn + bonus.reshape(b, t, B_DIM)) * g
    return o, pb[:, -1], _unpair_states(s_pair)


def _round_bf16(x):
    return x.astype(BF16).astype(F32)


def _conv_stage(u, st_ref, ns_ref, xbuf, *, halo):
    i = pl.program_id(1)
    rows = u.shape[0]

    @pl.when(i == 0)
    def _():
        xbuf[0:halo, :] = _round_bf16(st_ref[0])

    @pl.when(i > 0)
    def _():
        xbuf[0:halo, :] = xbuf[rows:rows + halo, :]

    xbuf[halo:halo + rows, :] = _round_bf16(u)

    @pl.when(i == pl.num_programs(1) - 1)
    def _():
        ns_ref[0] = u[rows - halo:, :]


def _conv_taps(xbuf, w_ref, rows, cols, *, width, halo):
    acc = None
    for j in range(width):
        term = xbuf[pl.ds(halo - (width - 1) + j, rows), cols] * _round_bf16(w_ref[j:j + 1, cols])
        acc = term if acc is None else acc + term
    return acc


def _sconv_kernel(bg_ref, cg_ref, hc_ref, st_ref, w_ref, o_ref, ns_ref, xbuf):
    rows = o_ref.shape[0]
    _conv_stage(cg_ref[...] * hc_ref[...], st_ref, ns_ref, xbuf, halo=CONV_C_HALO)
    for c0 in range(0, o_ref.shape[1], CONV_STRIP):
        cols = slice(c0, c0 + CONV_STRIP)
        y = _conv_taps(xbuf, w_ref, rows, cols, width=C_WIDTH, halo=CONV_C_HALO)
        o_ref[:, cols] = (bg_ref[:, cols] * y).astype(o_ref.dtype)


def _cconv_kernel(ga_ref, gb_ref, st_ref, w_ref, b_ref, g_ref, beta_ref, o_ref, ns_ref, xbuf, ybuf):
    rows = o_ref.shape[0]
    _conv_stage(ga_ref[...] * jax.nn.sigmoid(gb_ref[...]), st_ref, ns_ref, xbuf,
                halo=CONV_D_HALO)
    for c0 in range(0, o_ref.shape[1], CONV_STRIP):
        cols = slice(c0, c0 + CONV_STRIP)
        ybuf[:, cols] = _conv_taps(xbuf, w_ref, rows, cols, width=D_WIDTH, halo=CONV_D_HALO) + b_ref[:, cols]
    z = ybuf[...]
    mu = jnp.mean(z, axis=-1, keepdims=True)
    var = jnp.mean(jnp.square(z - mu), axis=-1, keepdims=True)
    zn = (z - mu) * lax.rsqrt(var + NORM_EPS) * g_ref[...] + beta_ref[...]
    o_ref[...] = (zn * jax.nn.sigmoid(zn)).astype(o_ref.dtype)


def _conv_call(kernel_fn, name, srcs, state, consts, *, width, halo, first_row, rows, extra_scratch):
    nseq = state.shape[0]
    d = D_MODEL
    hist = width - 1
    seq_len = N_PROMPT if nseq == 1 else DEC_SEQ
    assert seq_len % rows == 0 and rows >= halo >= hist
    tiles = seq_len // rows
    base = first_row // rows
    row_blk = lambda b, i: base + b * tiles + i
    in_specs = [pl.BlockSpec((rows, d), functools.partial(lambda b, i, c: (row_blk(b, i), c), c=col))
                for _, col in srcs]
    in_specs.append(pl.BlockSpec((1, halo, d), lambda b, i: (b, 0, 0)))
    in_specs += [pl.BlockSpec(c.shape, lambda b, i: (0, 0)) for c in consts]
    state_pad = jnp.pad(state, ((0, 0), (halo - hist, 0), (0, 0)))
    out, tail = pl.pallas_call(
        functools.partial(kernel_fn), name=name, grid=(nseq, tiles),
        in_specs=in_specs,
        out_specs=[pl.BlockSpec((rows, d), lambda b, i: (b * tiles + i, 0)),
                   pl.BlockSpec((1, halo, d), lambda b, i: (b, 0, 0))],
        out_shape=[jax.ShapeDtypeStruct((nseq * seq_len, d), BF16), jax.ShapeDtypeStruct((nseq, halo, d), F32)],
        scratch_shapes=[pltpu.VMEM((halo + rows, d), F32)] + extra_scratch(rows),
        compiler_params=_cparams(("parallel", "arbitrary")))(*[a for a, _ in srcs], state_pad, *consts)
    return out, tail[:, halo - hist:]


def short_conv(bch, state, w, *, first_row, rows):
    return _conv_call(_sconv_kernel, "sconv", [(bch, 0), (bch, 1), (bch, 2)], state, [w],
                      width=C_WIDTH, halo=CONV_C_HALO, first_row=first_row, rows=rows,
                      extra_scratch=lambda r: [])


def conformer_conv(pd, state, w, b, g, beta, *, first_row, rows):
    row = lambda z: z.reshape(1, D_MODEL)
    return _conv_call(_cconv_kernel, "cconv", [(pd, 0), (pd, 1)], state, [w, row(b), row(g), row(beta)],
                      width=D_WIDTH, halo=CONV_D_HALO, first_row=first_row, rows=rows,
                      extra_scratch=lambda r: [pltpu.VMEM((r, D_MODEL), F32)])


def _moe_up_kernel(be_ref, cnt_ref, x_ref, w_ref, b_ref, o_ref):
    i = pl.program_id(0)

    @pl.when(cnt_ref[i] > 0)
    def _():
        ff = o_ref.shape[1]
        tf = min(ff, MOE_FF_TILE)
        x = x_ref[...].astype(BF16)
        for f in range(ff // tf):
            gcol = slice(f * tf, (f + 1) * tf)
            ucol = slice(ff + f * tf, ff + (f + 1) * tf)
            gate = _dot(x, w_ref[0, 0, :, gcol]) + b_ref[0, 0, :, gcol]
            up = _dot(x, w_ref[0, 0, :, ucol]) + b_ref[0, 0, :, ucol]
            gate = jnp.minimum(gate, SWIGLU_LIMIT)
            up = jnp.clip(up, -SWIGLU_LIMIT, SWIGLU_LIMIT)
            act = gate * jax.nn.sigmoid(SWIGLU_ALPHA * gate) * (up + 1.0)
            o_ref[:, gcol] = act.astype(o_ref.dtype)

    @pl.when(cnt_ref[i] == 0)
    def _():
        o_ref[...] = jnp.zeros(o_ref.shape, o_ref.dtype)


def _row_copy(src_ref, src_row, dst_ref, dst_row, sem):
    return pltpu.make_async_copy(src_ref.at[pl.ds(src_row, 1)], dst_ref.at[pl.ds(dst_row, 1)], sem)


def _gather_rows_kernel(idx_ref, src_hbm, o_ref, sem):
    rows = o_ref.shape[0]

    def start_pair(i, carry):
        for prio in range(2):
            r = 2 * i + prio
            _row_copy(src_hbm, idx_ref[0, 0, r], o_ref, r, sem).start(priority=prio)
        return carry

    def wait(r, carry):
        _row_copy(src_hbm, 0, o_ref, r, sem).wait()
        return carry

    lax.fori_loop(0, rows // 2, start_pair, 0, unroll=DMA_UNROLL // 2)
    lax.fori_loop(0, rows, wait, 0, unroll=DMA_UNROLL)


def gather_rows(src, idx, blk):
    d = src.shape[1]
    nblk = idx.shape[0] // blk
    return pl.pallas_call(
        _gather_rows_kernel, name="moe_gather", grid=(nblk,),
        in_specs=[pl.BlockSpec((1, 1, blk), lambda i: (i, 0, 0), memory_space=pltpu.SMEM),
                  pl.BlockSpec(memory_space=pl.ANY)],
        out_specs=pl.BlockSpec((blk, d), lambda i: (i, 0)),
        out_shape=jax.ShapeDtypeStruct((nblk * blk, d), src.dtype),
        scratch_shapes=[pltpu.SemaphoreType.DMA(())],
        compiler_params=_cparams(("arbitrary",)))(idx.reshape(nblk, 1, blk), src)


def _moe_down_kernel(be_ref, cnt_ref, tgt_ref, a_ref, w_ref, b_ref, p_ref, y_hbm, ybuf, sem):
    i = pl.program_id(0)
    cnt = cnt_ref[i]

    @pl.when(cnt > 0)
    def _():
        ybuf[...] = (_dot(a_ref[...], w_ref[0, 0]) + b_ref[0, 0]) * p_ref[...]

        def start(r, carry):
            _row_copy(ybuf, r, y_hbm, tgt_ref[0, 0, r], sem).start()
            return carry

        def wait(r, carry):
            _row_copy(ybuf, r, y_hbm, 0, sem).wait()
            return carry

        full = ybuf.shape[0]

        def start_pair(i, carry):
            for prio in range(2):
                r = 2 * i + prio
                _row_copy(ybuf, r, y_hbm, tgt_ref[0, 0, r], sem).start(priority=prio)
            return carry

        @pl.when(cnt == full)
        def _():
            lax.fori_loop(0, full // 2, start_pair, 0, unroll=DMA_UNROLL // 2)
            lax.fori_loop(0, full, wait, 0, unroll=DMA_UNROLL)

        @pl.when(cnt < full)
        def _():
            lax.fori_loop(0, cnt, start, 0)
            lax.fori_loop(0, cnt, wait, 0)


def moe_experts(xb, row_p, row_tgt, block_e, block_cnt, n_pairs, layer, w_gu, b_gu, w_down, b_down):
    rows, d = xb.shape
    nblk = rows // MOE_BLK
    ff = w_down.shape[2]
    cp = pltpu.CompilerParams(dimension_semantics=("arbitrary",), vmem_limit_bytes=MOE_VMEM_LIMIT)
    act = pl.pallas_call(
        _moe_up_kernel, name="moe_up",
        grid_spec=pltpu.PrefetchScalarGridSpec(
            num_scalar_prefetch=2, grid=(nblk,),
            in_specs=[pl.BlockSpec((MOE_BLK, d), lambda i, be, cnt: (i, 0)),
                      pl.BlockSpec((1, 1, d, 2 * ff), lambda i, be, cnt: (layer, be[i], 0, 0)),
                      pl.BlockSpec((1, 1, 1, 2 * ff), lambda i, be, cnt: (layer, be[i], 0, 0))],
            out_specs=pl.BlockSpec((MOE_BLK, ff), lambda i, be, cnt: (i, 0))),
        out_shape=jax.ShapeDtypeStruct((rows, ff), BF16),
        compiler_params=cp)(block_e, block_cnt, xb, w_gu, b_gu)
    return pl.pallas_call(
        _moe_down_kernel, name="moe_down",
        grid_spec=pltpu.PrefetchScalarGridSpec(
            num_scalar_prefetch=2, grid=(nblk,),
            in_specs=[pl.BlockSpec((1, 1, MOE_BLK), lambda i, be, cnt: (i, 0, 0), memory_space=pltpu.SMEM),
                      pl.BlockSpec((MOE_BLK, ff), lambda i, be, cnt: (i, 0)),
                      pl.BlockSpec((1, 1, ff, d), lambda i, be, cnt: (layer, be[i], 0, 0)),
                      pl.BlockSpec((1, 1, 1, d), lambda i, be, cnt: (layer, be[i], 0, 0)),
                      pl.BlockSpec((MOE_BLK, 1), lambda i, be, cnt: (i, 0))],
            out_specs=pl.BlockSpec(memory_space=pl.ANY),
            scratch_shapes=[pltpu.VMEM((MOE_BLK, d), F32), pltpu.SemaphoreType.DMA(())]),
        out_shape=jax.ShapeDtypeStruct((n_pairs, d), F32),
        compiler_params=cp)(block_e, block_cnt, row_tgt.reshape(nblk, 1, MOE_BLK), act, w_down, b_down, row_p)


def moe_ffn(h, logits, layer, w_gu, b_gu, w_down, b_down):
    n, d = h.shape
    n_pairs = n * TOP_K
    top_val, top_idx = lax.top_k(logits, TOP_K)
    probs = jax.nn.softmax(top_val, axis=-1).reshape(-1)
    flat_e = top_idx.reshape(-1).astype(jnp.int32)
    order = jnp.argsort(flat_e, stable=True).astype(jnp.int32)
    counts = jnp.sum(flat_e[:, None] == jnp.arange(N_EXPERTS, dtype=jnp.int32)[None, :], axis=0, dtype=jnp.int32)
    padded = (counts + MOE_BLK - 1) // MOE_BLK * MOE_BLK
    pad_end = jnp.cumsum(padded)
    pad_start = pad_end - padded
    start = jnp.cumsum(counts) - counts
    nblk = -(-(n_pairs + N_EXPERTS * (MOE_BLK - 1)) // MOE_BLK)
    blk_row0 = jnp.arange(nblk, dtype=jnp.int32) * MOE_BLK
    block_e = jnp.minimum(jnp.sum(blk_row0[:, None] >= pad_end[None, :], axis=1, dtype=jnp.int32), N_EXPERTS - 1)
    blk_off = blk_row0 - pad_start[block_e]
    block_cnt = jnp.clip(counts[block_e] - blk_off, 0, MOE_BLK)
    lane = jnp.arange(MOE_BLK, dtype=jnp.int32)[None, :]
    valid = lane < block_cnt[:, None]
    src = jnp.where(valid, (start[block_e] + blk_off)[:, None] + lane, 0).reshape(-1)
    row_pair = jnp.where(valid.reshape(-1), order[src], 0)
    row_p = jnp.where(valid.reshape(-1), probs[row_pair], 0.0)
    row_tok = row_pair // TOP_K
    xb = gather_rows(h, row_tok, MOE_BLK)
    row_tgt = (row_pair % TOP_K) * n + row_tok
    return moe_experts(xb, row_p.reshape(-1, 1), row_tgt, block_e, block_cnt, n_pairs,
                       layer, w_gu, b_gu, w_down, b_down)


def _split_groups(z):
    return z[:N_PROMPT].reshape(1, SEQ, -1), z[N_PROMPT:].reshape(DEC_BATCH, DEC_SEQ, -1)


def _join_groups(zp, zs):
    return jnp.concatenate([zp.reshape(N_PROMPT, -1), zs.reshape(N_SAMPLE, -1)], axis=0)


def layer(x, mod_g, lp, st, attn_tabs):
    d = D_MODEL
    h = norm_mod(x, lp['norm1_g'], mod_g, 0, 1)
    qkv = matmul(h, lp['w_qkv'], name="proj_qkv", tm=PROJ_TM, tn=512)
    pb = matmul(h, lp['w_pb'], name="proj_rwkv", tm=PROJ_TM, tn=1280)
    bch = matmul(h, lp['w_bch'], name="proj_sconv", tm=PROJ_TM, tn=1024)
    pd = matmul(h, lp['w_pd'], name="proj_cconv", tm=PROJ_TM, tn=1024)
    gates = matmul(h, lp['w_gl'], name="proj_gates", tm=PROJ_TM, tn=1024, act="sigmoid")

    (bias_p, sink_p), (bias_s, sink_s) = attn_tabs
    o_a = jnp.concatenate([
        attn_prompt(qkv, bias_p, sink_p, N_PROMPT),
        attn_sample(qkv, st['k'].reshape(DEC_BATCH, WINDOW, A_KV), st['v'].reshape(DEC_BATCH, WINDOW, A_KV),
                    bias_s, sink_s, N_PROMPT)], axis=0)
    kp, ks = _split_groups(qkv[:, OFF_K:OFF_V])
    vp, vs = _split_groups(qkv[:, OFF_V:OFF_PB])
    new_kv_p = (kp[:, SEQ - WINDOW:].reshape(1, WINDOW, A_KV_HEADS, A_HEAD_DIM),
                vp[:, SEQ - WINDOW:].reshape(1, WINDOW, A_KV_HEADS, A_HEAD_DIM))
    new_kv_s = (ks.reshape(DEC_BATCH, DEC_SEQ, A_KV_HEADS, A_HEAD_DIM),
                vs.reshape(DEC_BATCH, DEC_SEQ, A_KV_HEADS, A_HEAD_DIM))

    pb_p, pb_s = _split_groups(pb)
    ob_p, shift_p, wkv_p = rwkv7_mix(pb_p, jnp.zeros((1, B_PROJ), F32),
                                     jnp.zeros((1, B_HEADS, B_HEAD, B_HEAD), F32), lp)
    ob_s, shift_s, wkv_s = rwkv7_mix(pb_s, st['shift'], st['wkv'], lp)
    o_b = _join_groups(ob_p, ob_s)

    ocp, sconv_p = short_conv(bch, jnp.zeros((1, C_WIDTH - 1, d), F32), lp['sconv_w'],
                              first_row=0, rows=CONV_ROWS)
    ocs, sconv_s = short_conv(bch, st['sconv'], lp['sconv_w'], first_row=N_PROMPT, rows=DEC_SEQ)
    o_c = jnp.concatenate([ocp, ocs], axis=0)
    cargs = (lp['cconv_w'], lp['cconv_b'], lp['cnorm_g'], lp['cnorm_b'])
    odp, cconv_p = conformer_conv(pd, jnp.zeros((1, D_WIDTH - 1, d), F32), *cargs,
                                  first_row=0, rows=CONV_ROWS)
    ods, cconv_s = conformer_conv(pd, st['cconv'], *cargs, first_row=N_PROMPT, rows=DEC_SEQ)
    o_d = jnp.concatenate([odp, ods], axis=0)

    merged = merge_branches([o_a, o_b.astype(BF16), o_c, o_d], lp['w_branch'], lp['layer'], gates)
    x = matmul_resid(merged, lp['w_out'], x, mod_g, 2)

    h2, logits = norm_mod(x, lp['norm2_g'], mod_g, 3, 4, router=(lp['router_w'], lp['router_b']))
    ffn = moe_ffn(h2, logits, lp['layer'], lp['w_gu'], lp['b_gu'], lp['w_down'], lp['b_down'])
    x = resid_gate(x, ffn, mod_g, 5)
    new_p = (new_kv_p[0], new_kv_p[1], wkv_p, shift_p, sconv_p, cconv_p)
    new_s = (new_kv_s[0], new_kv_s[1], wkv_s, shift_s, sconv_s, cconv_s)
    return x, new_p, new_s


def kernel(x_prompt, x_sample, c_prompt, c_sample, cache_k, cache_v, state_wkv, state_shift, state_sconv, state_cconv, w_mod, b_mod, norm1_g, norm2_g, w_in, attn_sinks, rel_bias_table, rwkv_mu, rwkv_w0, rwkv_w2, rwkv_a0, rwkv_a2, rwkv_g2, rwkv_k_k, rwkv_k_a, rwkv_r_k, rwkv_ln_g, rwkv_ln_b, sconv_w, cconv_w, cconv_b, cnorm_g, cnorm_b, w_branch, w_out, router_w, router_b, expert_w_gu, expert_b_gu, expert_w_down, expert_b_down, final_g):
    d = D_MODEL
    x = jnp.concatenate([x_prompt.reshape(N_PROMPT, d), x_sample.reshape(N_SAMPLE, d)], axis=0)
    n_seq = BATCH + DEC_BATCH
    c_act = jnp.pad(jnp.concatenate([jax.nn.silu(c_prompt), jax.nn.silu(c_sample)], axis=0),
                    ((0, 16 - n_seq), (0, 0)))

    blk_pos = jnp.arange(CHUNK, dtype=jnp.int32)
    qpos_p = WINDOW + blk_pos
    kpos_p = jnp.arange(WINDOW + CHUNK, dtype=jnp.int32)
    qpos_s = PAST_LEN + jnp.arange(DEC_SEQ, dtype=jnp.int32)
    kpos_s = PAST_LEN - WINDOW + jnp.arange(WINDOW + DEC_SEQ, dtype=jnp.int32)

    w_branch_bf = w_branch.astype(BF16)
    w_gu_bf = expert_w_gu.astype(BF16)
    w_down_bf = expert_w_down.astype(BF16)
    b_gu4 = expert_b_gu.reshape(DEPTH, N_EXPERTS, 1, -1)
    b_down4 = expert_b_down.reshape(DEPTH, N_EXPERTS, 1, -1)

    st_p, st_s = [], []
    for l in range(DEPTH):
        w_in_l = w_in[l]
        lp = dict(norm1_g=norm1_g[l], norm2_g=norm2_g[l],
                  w_qkv=w_in_l[:, OFF_Q:OFF_PB].astype(BF16), w_pb=w_in_l[:, OFF_PB:OFF_BG].astype(BF16),
                  w_bch=w_in_l[:, OFF_BG:OFF_PD].astype(BF16), w_pd=w_in_l[:, OFF_PD:OFF_GL].astype(BF16),
                  w_gl=w_in_l[:, OFF_GL:].astype(BF16),
                  mu=rwkv_mu[l], w0=rwkv_w0[l], w2=rwkv_w2[l], a0=rwkv_a0[l], a2=rwkv_a2[l],
                  g2=rwkv_g2[l], k_k=rwkv_k_k[l], k_a=rwkv_k_a[l], r_k=rwkv_r_k[l], ln_g=rwkv_ln_g[l],
                  ln_b=rwkv_ln_b[l], sconv_w=sconv_w[l], cconv_w=cconv_w[l], cconv_b=cconv_b[l],
                  cnorm_g=cnorm_g[l], cnorm_b=cnorm_b[l], layer=l, w_branch=w_branch_bf,
                  w_out=w_out[l].astype(BF16), router_w=router_w[l], router_b=router_b[l],
                  w_gu=w_gu_bf, b_gu=b_gu4, w_down=w_down_bf, b_down=b_down4)
        mod = matmul(c_act, w_mod[l].astype(BF16), name="modulation", tm=16, tn=512)[:n_seq] + b_mod[l]
        mod_g = jnp.concatenate([jnp.broadcast_to(mod[:1], (N_PROMPT // MOD_GROUP, 6 * d)),
                                 jnp.repeat(mod[1:], DEC_SEQ // MOD_GROUP, axis=0)], axis=0)
        tabs = (_attn_tables(qpos_p, kpos_p, rel_bias_table, attn_sinks[l], False),
                _attn_tables(qpos_s, kpos_s, rel_bias_table, attn_sinks[l], True))
        st = dict(k=cache_k[l], v=cache_v[l], shift=state_shift[l], wkv=state_wkv[l],
                  sconv=state_sconv[l], cconv=state_cconv[l])
        x, sp, ss = layer(x, mod_g, lp, st, tabs)
        st_p.append(sp)
        st_s.append(ss)
    y = final_norm(x, final_g)
    y_prompt = y[:N_PROMPT].reshape(1, SEQ, d)
    y_sample = y[N_PROMPT:].reshape(DEC_BATCH, DEC_SEQ, d)
    return (y_prompt, y_sample,
            jnp.stack([s[0] for s in st_p]), jnp.stack([s[1] for s in st_p]), jnp.stack([s[2] for s in st_p]),
            jnp.stack([s[3] for s in st_p]), jnp.stack([s[4] for s in st_p]), jnp.stack([s[5] for s in st_p]),
            jnp.stack([s[0] for s in st_s]), jnp.stack([s[1] for s in st_s]), jnp.stack([s[2] for s in st_s]),
            jnp.stack([s[3] for s in st_s]), jnp.stack([s[4] for s in st_s]), jnp.stack([s[5] for s in st_s]))
```
